```python
import math
import jax, jax.numpy as jnp
from jax import lax
import numpy as np

D_MODEL = 2048
BATCH = 4
SEQ = 4096
DEPTH = 1

MEM_LEN = 256
GRID_W = 64
HEAD_DIM = 128
N_Q_HEADS = D_MODEL // HEAD_DIM
N_KV_HEADS = N_Q_HEADS // 4
GQA_REP = N_Q_HEADS // N_KV_HEADS
ROPE_AXIS_DIM = HEAD_DIM // 2
ROPE_THETA = 10000.0
Q_BLOCK = 128
SSD_EXPAND = 2
D_INNER = SSD_EXPAND * D_MODEL
SSD_HEAD_DIM = 64
N_SSD_HEADS = D_INNER // SSD_HEAD_DIM
N_SSD_GROUPS = 8
D_STATE = 128
CONV_W = 5
CHUNK = 128
DT_MIN = 0.001
DT_MAX = 0.1
A_MIN = 1.0
A_MAX = 16.0
N_MEM_HEADS = 4
MEM_HEAD_DIM = 128
D_FF = 5632
LN_EPS = 1e-5
RMS_EPS = 1e-6
Q_W = N_Q_HEADS * HEAD_DIM
KV_W = N_KV_HEADS * HEAD_DIM
BC_W = N_SSD_GROUPS * D_STATE
XBC_W = D_INNER + 2 * BC_W
DT_W = 2 * N_SSD_HEADS
GATE_W = 2 * D_MODEL
IN_W = Q_W + 2 * KV_W + D_INNER + XBC_W + DT_W + GATE_W
MEM_W = N_MEM_HEADS * MEM_HEAD_DIM

kernel_name = 'hybrid_gqa_ssd_macaron_deepnorm_encoder'


def layer_norm(x, g, b):
    xf = x.astype(jnp.float32)
    mu = jnp.mean(xf, axis=-1, keepdims=True)
    var = jnp.mean(jnp.square(xf - mu), axis=-1, keepdims=True)
    y = (xf - mu) * lax.rsqrt(var + LN_EPS)
    return (y * g.astype(jnp.float32) + b.astype(jnp.float32)).astype(x.dtype)


def rms_norm(x, g):
    xf = x.astype(jnp.float32)
    y = xf * lax.rsqrt(jnp.mean(jnp.square(xf), axis=-1, keepdims=True) + RMS_EPS)
    return (y * g.astype(jnp.float32)).astype(x.dtype)


def swiglu(x, w1, w3, w2):
    return (jax.nn.silu(x @ w1) * (x @ w3)) @ w2


def axial_rope_tables(seq_len):
    rows = seq_len // GRID_W
    row = jnp.repeat(jnp.arange(rows), GRID_W).astype(jnp.float32)
    col = jnp.tile(jnp.arange(GRID_W), rows).astype(jnp.float32)
    inv_freq = ROPE_THETA ** (-jnp.arange(ROPE_AXIS_DIM // 2, dtype=jnp.float32) * 2.0 / ROPE_AXIS_DIM)
    ang_r = row[:, None] * inv_freq[None, :]
    ang_c = col[:, None] * inv_freq[None, :]
    return jnp.cos(ang_r), jnp.sin(ang_r), jnp.cos(ang_c), jnp.sin(ang_c)


def rotate_half_axis(x, cos, sin):
    x1, x2 = jnp.split(x, 2, axis=-1)
    c = cos[None, :, None, :].astype(x.dtype)
    s = sin[None, :, None, :].astype(x.dtype)
    return jnp.concatenate([x1 * c - x2 * s, x2 * c + x1 * s], axis=-1)


def apply_axial_rope(x, tables):
    cos_r, sin_r, cos_c, sin_c = tables
    x_row, x_col = x[..., :ROPE_AXIS_DIM], x[..., ROPE_AXIS_DIM:]
    return jnp.concatenate([rotate_half_axis(x_row, cos_r, sin_r),
                            rotate_half_axis(x_col, cos_c, sin_c)], axis=-1)


def block_attention(q, k, v):
    b, s = q.shape[0], q.shape[1]
    n_blk = s // Q_BLOCK
    scale = HEAD_DIM ** -0.5
    qb = q.reshape(b, n_blk, Q_BLOCK, N_KV_HEADS, GQA_REP, HEAD_DIM).transpose(1, 0, 2, 3, 4, 5)

    def attend(q_blk):
        sc = jnp.einsum('bqkrd,bskd->bkrqs', q_blk, k).astype(jnp.float32) * scale
        p = jax.nn.softmax(sc, axis=-1).astype(v.dtype)
        return jnp.einsum('bkrqs,bskd->bqkrd', p, v)

    out = lax.map(attend, qb)
    return out.transpose(1, 0, 2, 3, 4, 5).reshape(b, s, Q_W)


def centred_depthwise_conv(x, w, bias):
    pad = CONV_W // 2
    y = lax.conv_general_dilated(x, w[:, None, :].astype(x.dtype), window_strides=(1,),
                                 padding=[(pad, pad)], dimension_numbers=('NWC', 'WIO', 'NWC'),
                                 feature_group_count=x.shape[-1])
    return y + bias


def ssd_chunked(x, dt, A, Bm, Cm, include_current):
    b, l, h, p = x.shape
    g, n = Bm.shape[2], Bm.shape[3]
    r = h // g
    nc = l // CHUNK
    xd = (x * dt[..., None]).reshape(b, nc, CHUNK, g, r, p)
    a = (dt * A).reshape(b, nc, CHUNK, g, r).transpose(0, 3, 4, 1, 2)
    a_cs = jnp.cumsum(a, axis=-1)
    Bc = Bm.reshape(b, nc, CHUNK, g, n)
    Cc = Cm.reshape(b, nc, CHUNK, g, n)
    idx = jnp.arange(CHUNK)
    mask = (idx[:, None] >= idx[None, :]) if include_current else (idx[:, None] > idx[None, :])
    seg = a_cs[..., :, None] - a_cs[..., None, :]
    decay = jnp.exp(jnp.where(mask, seg, -jnp.inf))
    cb = jnp.einsum('bclgn,bcsgn->bgcls', Cc, Bc)
    y_diag = jnp.einsum('bgcls,bgrcls,bcsgrp->bclgrp', cb, decay, xd)
    decay_to_end = jnp.exp(a_cs[..., -1:] - a_cs)
    states = jnp.einsum('bcsgn,bgrcs,bcsgrp->bcgrpn', Bc, decay_to_end, xd)
    a_chunk = a_cs[..., -1]
    cs = jnp.cumsum(a_chunk, axis=-1)
    cs_excl = cs - a_chunk
    cidx = jnp.arange(nc)
    cmask = cidx[:, None] > cidx[None, :]
    chunk_decay = jnp.exp(jnp.where(cmask, cs_excl[..., :, None] - cs[..., None, :], -jnp.inf))
    states_in = jnp.einsum('bgrzc,bcgrpn->bzgrpn', chunk_decay, states)
    y_off = jnp.einsum('bclgn,bcgrpn,bgrcl->bclgrp', Cc, states_in, jnp.exp(a_cs))
    return (y_diag + y_off).reshape(b, l, h, p)


def ssd_bidirectional(xs, bm, cm, z, dt_raw, dt_bias, a_log, ssd_d, ssd_norm_g):
    b, s, _ = xs.shape
    f32 = jnp.float32
    xh = xs.reshape(b, s, N_SSD_HEADS, SSD_HEAD_DIM).astype(f32)
    bm = bm.reshape(b, s, N_SSD_GROUPS, D_STATE).astype(f32)
    cm = cm.reshape(b, s, N_SSD_GROUPS, D_STATE).astype(f32)
    dt = jax.nn.softplus(dt_raw.astype(f32).reshape(b, s, 2, N_SSD_HEADS) + dt_bias.astype(f32))
    A = -jnp.exp(a_log.astype(f32))
    flip = lambda t: t[:, ::-1]
    y_fwd = ssd_chunked(xh, dt[:, :, 0], A[0], bm, cm, True)
    y_bwd = flip(ssd_chunked(flip(xh), flip(dt[:, :, 1]), A[1], flip(bm), flip(cm), False))
    y = y_fwd + y_bwd + ssd_d.astype(f32)[:, None] * xh
    y = y.reshape(b, s, D_INNER) * jax.nn.silu(z.astype(f32))
    y = y.reshape(b, s, N_SSD_GROUPS, D_INNER // N_SSD_GROUPS)
    y = y * lax.rsqrt(jnp.mean(jnp.square(y), axis=-1, keepdims=True) + RMS_EPS)
    y = y.reshape(b, s, D_INNER) * ssd_norm_g.astype(f32)
    return y.astype(xs.dtype)


def mixer_block(xn, w_in, q_norm_g, k_norm_g, conv_w, conv_b, dt_bias, a_log, ssd_d, ssd_norm_g,
                w_branch_attn, w_branch_ssd, w_mix_out):
    b, s, _ = xn.shape
    h = xn @ w_in
    offs = [int(o) for o in np.cumsum([Q_W, KV_W, KV_W, D_INNER, XBC_W, DT_W])]
    q, k, v, z, xbc, dt_raw, gate = jnp.split(h, offs, axis=-1)
    tables = axial_rope_tables(s)
    q = apply_axial_rope(rms_norm(q.reshape(b, s, N_Q_HEADS, HEAD_DIM), q_norm_g), tables)
    k = apply_axial_rope(rms_norm(k.reshape(b, s, N_KV_HEADS, HEAD_DIM), k_norm_g), tables)
    v = v.reshape(b, s, N_KV_HEADS, HEAD_DIM)
    y_attn = block_attention(q.reshape(b, s, N_KV_HEADS, GQA_REP, HEAD_DIM), k, v)
    xbc = jax.nn.silu(centred_depthwise_conv(xbc, conv_w, conv_b))
    xs, bm, cm = jnp.split(xbc, [D_INNER, D_INNER + BC_W], axis=-1)
    y_ssd = ssd_bidirectional(xs, bm, cm, z, dt_raw, dt_bias, a_log, ssd_d, ssd_norm_g)
    gates = jax.nn.sigmoid(gate.astype(jnp.float32)).astype(xn.dtype)
    g_attn, g_ssd = jnp.split(gates, 2, axis=-1)
    merged = g_attn * (y_attn @ w_branch_attn) + g_ssd * (y_ssd @ w_branch_ssd)
    return merged @ w_mix_out


def memory_cross_attention(xn, mem, wq, wk, wv, wo):
    b, s, _ = xn.shape
    m = mem.shape[1]
    q = (xn @ wq).reshape(b, s, N_MEM_HEADS, MEM_HEAD_DIM)
    k = (mem @ wk).reshape(b, m, N_MEM_HEADS, MEM_HEAD_DIM)
    v = (mem @ wv).reshape(b, m, N_MEM_HEADS, MEM_HEAD_DIM)
    sc = jnp.einsum('bqhd,bmhd->bhqm', q, k).astype(jnp.float32) * MEM_HEAD_DIM ** -0.5
    p = jax.nn.softmax(sc, axis=-1).astype(v.dtype)
    o = jnp.einsum('bhqm,bmhd->bqhd', p, v).reshape(b, s, MEM_W)
    return o @ wo


def setup_inputs(seed: int = 0) -> dict:
    key = jax.random.key(seed)
    ks = jax.random.split(key, 32)
    beta = (8.0 * DEPTH) ** -0.25

    def nrm(k, shape, scale):
        return scale * jax.random.normal(k, shape, jnp.float32)

    def dense(k, fan_in, fan_out, gain=1.0):
        return nrm(k, (DEPTH, fan_in, fan_out), gain * fan_in ** -0.5)

    def ones_noise(k, n):
        return 1.0 + nrm(k, (DEPTH, n), 0.01)

    def small(k, n):
        return nrm(k, (DEPTH, n), 0.01)

    u_dt = jax.random.uniform(ks[12], (DEPTH, 2, N_SSD_HEADS), jnp.float32)
    dt0 = jnp.exp(u_dt * (math.log(DT_MAX) - math.log(DT_MIN)) + math.log(DT_MIN))
    dt_bias = dt0 + jnp.log(-jnp.expm1(-dt0))
    a_log = jnp.log(jax.random.uniform(ks[13], (DEPTH, 2, N_SSD_HEADS), jnp.float32, A_MIN, A_MAX))
    return {
        'x': nrm(ks[0], (BATCH, SEQ, D_MODEL), 1.0),
        'mem': nrm(ks[1], (BATCH, MEM_LEN, D_MODEL), 1.0),
        'ffn1_w1': dense(ks[2], D_MODEL, D_FF),
        'ffn1_w3': dense(ks[3], D_MODEL, D_FF),
        'ffn1_w2': dense(ks[4], D_FF, D_MODEL, beta),
        'ln1_g': ones_noise(ks[5], D_MODEL),
        'ln1_b': small(ks[6], D_MODEL),
        'w_in': dense(ks[7], D_MODEL, IN_W),
        'q_norm_g': ones_noise(ks[8], HEAD_DIM),
        'k_norm_g': ones_noise(ks[9], HEAD_DIM),
        'conv_w': nrm(ks[10], (DEPTH, CONV_W, XBC_W), CONV_W ** -0.5),
        'conv_b': small(ks[11], XBC_W),
        'dt_bias': dt_bias,
        'a_log': a_log,
        'ssd_d': ones_noise(ks[14], N_SSD_HEADS),
        'ssd_norm_g': ones_noise(ks[15], D_INNER),
        'w_branch_attn': dense(ks[16], Q_W, D_MODEL),
        'w_branch_ssd': dense(ks[17], D_INNER, D_MODEL),
        'w_mix_out': dense(ks[18], D_MODEL, D_MODEL, beta),
        'ln2_g': ones_noise(ks[19], D_MODEL),
        'ln2_b': small(ks[20], D_MODEL),
        'mem_wq': dense(ks[21], D_MODEL, MEM_W),
        'mem_wk': dense(ks[22], D_MODEL, MEM_W),
        'mem_wv': dense(ks[23], D_MODEL, MEM_W),
        'mem_wo': dense(ks[24], MEM_W, D_MODEL, beta),
        'ln3_g': ones_noise(ks[25], D_MODEL),
        'ln3_b': small(ks[26], D_MODEL),
        'ffn2_w1': dense(ks[27], D_MODEL, D_FF),
        'ffn2_w3': dense(ks[28], D_MODEL, D_FF),
        'ffn2_w2': dense(ks[29], D_FF, D_MODEL, beta),
        'ln4_g': ones_noise(ks[30], D_MODEL),
        'ln4_b': small(ks[31], D_MODEL),
    }


def reference(x, mem, ffn1_w1, ffn1_w3, ffn1_w2, ln1_g, ln1_b, w_in, q_norm_g, k_norm_g, conv_w, conv_b,
              dt_bias, a_log, ssd_d, ssd_norm_g, w_branch_attn, w_branch_ssd, w_mix_out, ln2_g, ln2_b,
              mem_wq, mem_wk, mem_wv, mem_wo, ln3_g, ln3_b, ffn2_w1, ffn2_w3, ffn2_w2, ln4_g, ln4_b):
    alpha = (2.0 * DEPTH) ** 0.25
    for l in range(DEPTH):
        x = layer_norm(alpha * x + 0.5 * swiglu(x, ffn1_w1[l], ffn1_w3[l], ffn1_w2[l]), ln1_g[l], ln1_b[l])
        mix = mixer_block(x, w_in[l], q_norm_g[l], k_norm_g[l], conv_w[l], conv_b[l], dt_bias[l], a_log[l],
                          ssd_d[l], ssd_norm_g[l], w_branch_attn[l], w_branch_ssd[l], w_mix_out[l])
        x = layer_norm(alpha * x + mix, ln2_g[l], ln2_b[l])
        x = layer_norm(alpha * x + memory_cross_attention(x, mem, mem_wq[l], mem_wk[l], mem_wv[l], mem_wo[l]),
                       ln3_g[l], ln3_b[l])
        x = layer_norm(alpha * x + 0.5 * swiglu(x, ffn2_w1[l], ffn2_w3[l], ffn2_w2[l]), ln4_g[l], ln4_b[l])
    return x
```

```python
import functools
import math

import jax
import jax.numpy as jnp
from jax import lax
from jax.experimental import pallas as pl
from jax.experimental.pallas import tpu as pltpu

F32 = jnp.float32
BF16 = jnp.bfloat16

D_MODEL = 2048
DEPTH = 1
GRID_W = 64
HEAD_DIM = 128
N_Q_HEADS = 16
N_KV_HEADS = 4
GQA_REP = 4
ROPE_AXIS_DIM = 64
ROPE_THETA = 10000.0
D_INNER = 4096
SSD_HEAD_DIM = 64
N_SSD_HEADS = 64
N_SSD_GROUPS = 8
HEADS_PER_GROUP = N_SSD_HEADS // N_SSD_GROUPS
GROUP_W = D_INNER // N_SSD_GROUPS
D_STATE = 128
CONV_W = 5
CHUNK = 128
N_MEM_HEADS = 4
MEM_HEAD_DIM = 128
D_FF = 5632
LN_EPS = 1e-5
RMS_EPS = 1e-6
Q_W = 2048
KV_W = 512
BC_W = 1024
XBC_W = 6144
DT_W = 128
MEM_W = 512
ALPHA = (2.0 * DEPTH) ** 0.25

LANE = 128
COL_BLK = 512
QKV_BLKS = (Q_W + 2 * KV_W) // COL_BLK
Z_BLK0 = QKV_BLKS
XBC_BLK0 = Z_BLK0 + D_INNER // COL_BLK
GATE_BLK0 = XBC_BLK0 + XBC_W // COL_BLK
H_BLKS = GATE_BLK0 + 2 * D_MODEL // COL_BLK
H_W = H_BLKS * COL_BLK

VMEM_LIMIT = 56 * 1024 * 1024


def _cparams(sem):
    return pltpu.CompilerParams(dimension_semantics=sem, vmem_limit_bytes=VMEM_LIMIT)


def _dot(a, b):
    return jnp.dot(a, b, preferred_element_type=F32)


def _dot_nt(a, b):
    return lax.dot_general(a, b, (((1,), (1,)), ((), ())), preferred_element_type=F32)


def _dot_tn(a, b):
    return lax.dot_general(a, b, (((0,), (0,)), ((), ())), preferred_element_type=F32)


def _split3(v):
    hi = v.astype(BF16)
    r1 = v - hi.astype(F32)
    mid = r1.astype(BF16)
    lo = (r1 - mid.astype(F32)).astype(BF16)
    return hi, mid, lo


def _dot_exact_r(v, m01):
    hi, mid, lo = _split3(v)
    return _dot(hi, m01) + _dot(mid, m01) + _dot(lo, m01)


def _dot_exact_l(m01, v):
    hi, mid, lo = _split3(v)
    return _dot(m01, hi) + _dot(m01, mid) + _dot(m01, lo)


def _layer_norm(y, g, b):
    mu = jnp.mean(y, axis=-1, keepdims=True)
    d = y - mu
    var = jnp.mean(d * d, axis=-1, keepdims=True)
    return d * lax.rsqrt(var + LN_EPS) * g + b


def _silu(x):
    return x * jax.nn.sigmoid(x)


def _ffn_ln_kernel(x_ref, w1_ref, w3_ref, w2_ref, g_ref, b_ref, *rest, emit_bf16):
    if emit_bf16:
        o_ref, ob_ref, xb_ref, acc_ref = rest
    else:
        o_ref, xb_ref, acc_ref = rest
        ob_ref = None
    f = pl.program_id(1)

    @pl.when(f == 0)
    def _():
        xb_ref[...] = x_ref[...].astype(BF16)
        acc_ref[...] = jnp.zeros_like(acc_ref)

    xb = xb_ref[...]
    h1 = _dot(xb, w1_ref[...])
    h3 = _dot(xb, w3_ref[...])
    a = (_silu(h1) * h3).astype(BF16)
    acc_ref[...] += _dot(a, w2_ref[...])

    @pl.when(f == pl.num_programs(1) - 1)
    def _():
        y = ALPHA * x_ref[...] + 0.5 * acc_ref[...]
        o = _layer_norm(y, g_ref[...], b_ref[...])
        o_ref[...] = o
        if ob_ref is not None:
            ob_ref[...] = o.astype(BF16)


def _ffn_ln(x, w1, w3, w2, g, b, *, emit_bf16, tm=512, tf=512):
    m, d = x.shape
    dff = w1.shape[1]
    out_shape = [jax.ShapeDtypeStruct((m, d), F32)]
    out_specs = [pl.BlockSpec((tm, d), lambda i, f: (i, 0))]
    if emit_bf16:
        out_shape.append(jax.ShapeDtypeStruct((m, d), BF16))
        out_specs.append(pl.BlockSpec((tm, d), lambda i, f: (i, 0)))
    return pl.pallas_call(
        functools.partial(_ffn_ln_kernel, emit_bf16=emit_bf16),
        grid=(m // tm, dff // tf),
        in_specs=[
            pl.BlockSpec((tm, d), lambda i, f: (i, 0)),
            pl.BlockSpec((d, tf), lambda i, f: (0, f)),
            pl.BlockSpec((d, tf), lambda i, f: (0, f)),
            pl.BlockSpec((tf, d), lambda i, f: (f, 0)),
            pl.BlockSpec((1, d), lambda i, f: (0, 0)),
            pl.BlockSpec((1, d), lambda i, f: (0, 0)),
        ],
        out_specs=out_specs,
        out_shape=out_shape,
        scratch_shapes=[pltpu.VMEM((tm, d), BF16), pltpu.VMEM((tm, d), F32)],
        compiler_params=_cparams(("parallel", "arbitrary")),
        name="ffn_ln",
    )(x, w1, w3, w2, g, b)


def _in_proj_kernel(x_ref, w_ref, gain_ref, rc_ref, ra_ref, rb_ref, o_ref):
    n = pl.program_id(1)
    acc = _dot(x_ref[...], w_ref[...])

    @pl.when(n < QKV_BLKS - 1)
    def _():
        rc = rc_ref[...]
        ra = ra_ref[...]
        rb = rb_ref[...]
        for hh in range(COL_BLK // HEAD_DIM):
            sl = slice(hh * HEAD_DIM, (hh + 1) * HEAD_DIM)
            xh = acc[:, sl]
            ms = jnp.mean(xh * xh, axis=-1, keepdims=True)
            xh = xh * lax.rsqrt(ms + RMS_EPS) * gain_ref[:, sl]
            r = xh * rc + pltpu.roll(xh, 96, 1) * ra + pltpu.roll(xh, 32, 1) * rb
            o_ref[:, sl] = r.astype(BF16)

    @pl.when(jnp.logical_and(n >= QKV_BLKS - 1, n < GATE_BLK0))
    def _():
        o_ref[...] = acc.astype(BF16)

    @pl.when(n >= GATE_BLK0)
    def _():
        o_ref[...] = jax.nn.sigmoid(acc).astype(BF16)


def _in_proj(xb, w, gains, rc, ra, rb, *, seq, tm=1024):
    m, d = xb.shape
    tm = min(tm, seq)
    t_per_seq = seq // tm
    n_gain_blk = gains.shape[1] // COL_BLK
    return pl.pallas_call(
        _in_proj_kernel,
        grid=(m // tm, H_BLKS),
        in_specs=[
            pl.BlockSpec((tm, d), lambda i, n: (i, 0)),
            pl.BlockSpec((d, COL_BLK), lambda i, n: (0, n)),
            pl.BlockSpec((1, COL_BLK), lambda i, n: (0, jnp.minimum(n, n_gain_blk - 1))),
            pl.BlockSpec((tm, HEAD_DIM), lambda i, n: (i % t_per_seq, 0)),
            pl.BlockSpec((tm, HEAD_DIM), lambda i, n: (i % t_per_seq, 0)),
            pl.BlockSpec((tm, HEAD_DIM), lambda i, n: (i % t_per_seq, 0)),
        ],
        out_specs=pl.BlockSpec((tm, COL_BLK), lambda i, n: (i, n)),
        out_shape=jax.ShapeDtypeStruct((m, H_W), BF16),
        compiler_params=_cparams(("parallel", "arbitrary")),
        name="in_proj",
    )(xb, w, gains, rc, ra, rb)


def _matmul_kernel(x_ref, w_ref, o_ref):
    o_ref[...] = _dot(x_ref[...], w_ref[...]).astype(o_ref.dtype)


def _matmul(xb, w, out_dtype, *, tm, tn):
    m, d = xb.shape
    n = w.shape[1]
    tm = min(tm, m)
    return pl.pallas_call(
        _matmul_kernel,
        grid=(m // tm, n // tn),
        in_specs=[pl.BlockSpec((tm, d), lambda i, j: (i, 0)),
                  pl.BlockSpec((d, tn), lambda i, j: (0, j))],
        out_specs=pl.BlockSpec((tm, tn), lambda i, j: (i, j)),
        out_shape=jax.ShapeDtypeStruct((m, n), out_dtype),
        compiler_params=_cparams(("parallel", "arbitrary")),
        name="matmul",
    )(xb, w)


def _attn_kernel(q_ref, k_ref, v_ref, o_ref, *, tq, tk):
    seq = k_ref.shape[0]
    q = jnp.concatenate([q_ref[:, hh * HEAD_DIM:(hh + 1) * HEAD_DIM] for hh in range(GQA_REP)], axis=0)
    rows = GQA_REP * tq

    def body(j, carry):
        m_i, l_i, acc = carry
        start = pl.multiple_of(j * tk, tk)
        k = k_ref[pl.ds(start, tk), :]
        v = v_ref[pl.ds(start, tk), :]
        s = _dot_nt(q, k)
        m_new = jnp.maximum(m_i, jnp.max(s, axis=-1, keepdims=True))
        corr = jnp.exp(m_i - m_new)
        p = jnp.exp(s - m_new)
        l_new = corr * l_i + jnp.sum(p, axis=-1, keepdims=True)
        acc_new = corr * acc + _dot(p.astype(BF16), v)
        return m_new, l_new, acc_new

    m0 = jnp.full((rows, 1), -jnp.inf, F32)
    l0 = jnp.zeros((rows, 1), F32)
    a0 = jnp.zeros((rows, HEAD_DIM), F32)
    _, l_f, acc = lax.fori_loop(0, seq // tk, body, (m0, l0, a0))
    o = (acc / l_f).astype(BF16)
    for hh in range(GQA_REP):
        o_ref[:, hh * HEAD_DIM:(hh + 1) * HEAD_DIM] = o[hh * tq:(hh + 1) * tq]


def _attention(h3, *, tq=128, tk=512):
    bsz, seq, _ = h3.shape
    tk = min(tk, seq)
    k_blk0 = Q_W // HEAD_DIM
    v_blk0 = (Q_W + KV_W) // HEAD_DIM
    return pl.pallas_call(
        functools.partial(_attn_kernel, tq=tq, tk=tk),
        grid=(bsz, N_KV_HEADS, seq // tq),
        in_specs=[
            pl.BlockSpec((None, tq, GQA_REP * HEAD_DIM), lambda b, kv, i: (b, i, kv)),
            pl.BlockSpec((None, seq, HEAD_DIM), lambda b, kv, i: (b, 0, k_blk0 + kv)),
            pl.BlockSpec((None, seq, HEAD_DIM), lambda b, kv, i: (b, 0, v_blk0 + kv)),
        ],
        out_specs=pl.BlockSpec((None, tq, GQA_REP * HEAD_DIM), lambda b, kv, i: (b, i, kv)),
        out_shape=jax.ShapeDtypeStruct((bsz, seq, Q_W), BF16),
        compiler_params=_cparams(("parallel", "parallel", "arbitrary")),
        name="attention",
    )(h3, h3, h3)


CONV_HALO = 8
CONV_SUB = 64


def _conv_kernel(prev_ref, main_ref, next_ref, w_ref, b_ref, o_ref, win_ref, *, rt):
    r = pl.program_id(1)
    first = r == 0
    last = r == pl.num_programs(1) - 1
    win_ref[0:CONV_HALO, :] = jnp.where(first, 0.0, prev_ref[...].astype(F32))
    win_ref[CONV_HALO:CONV_HALO + rt, :] = main_ref[...].astype(F32)
    win_ref[CONV_HALO + rt:, :] = jnp.where(last, 0.0, next_ref[...].astype(F32))
    pad = CONV_W // 2
    bias = b_ref[...]
    for t in range(rt // CONV_SUB):
        base = CONV_HALO + t * CONV_SUB - pad
        acc = bias + w_ref[0:1, :] * win_ref[base:base + CONV_SUB, :]
        for k in range(1, CONV_W):
            acc = acc + w_ref[k:k + 1, :] * win_ref[base + k:base + k + CONV_SUB, :]
        o_ref[t * CONV_SUB:(t + 1) * CONV_SUB, :] = _silu(acc).astype(BF16)


def _conv_silu(h3, conv_w, conv_b, *, rt=512):
    bsz, seq, _ = h3.shape
    rt = min(rt, seq)
    nh = rt // CONV_HALO
    n_halo_blk = seq // CONV_HALO
    return pl.pallas_call(
        functools.partial(_conv_kernel, rt=rt),
        grid=(bsz, seq // rt, XBC_W // COL_BLK),
        in_specs=[
            pl.BlockSpec((None, CONV_HALO, COL_BLK),
                         lambda b, r, c: (b, jnp.maximum(r * nh - 1, 0), XBC_BLK0 + c)),
            pl.BlockSpec((None, rt, COL_BLK), lambda b, r, c: (b, r, XBC_BLK0 + c)),
            pl.BlockSpec((None, CONV_HALO, COL_BLK),
                         lambda b, r, c: (b, jnp.minimum((r + 1) * nh, n_halo_blk - 1), XBC_BLK0 + c)),
            pl.BlockSpec((CONV_W, COL_BLK), lambda b, r, c: (0, c)),
            pl.BlockSpec((1, COL_BLK), lambda b, r, c: (0, c)),
        ],
        out_specs=pl.BlockSpec((None, rt, COL_BLK), lambda b, r, c: (b, r, c)),
        out_shape=jax.ShapeDtypeStruct((bsz, seq, XBC_W), BF16),
        scratch_shapes=[pltpu.VMEM((rt + 2 * CONV_HALO, COL_BLK), F32)],
        compiler_params=_cparams(("parallel", "parallel", "arbitrary")),
        name="conv_silu",
    )(h3, h3, h3, conv_w, conv_b)


def _softplus(x):
    return jnp.maximum(x, 0.0) + jnp.log1p(jnp.exp(-jnp.abs(x)))


def _ssd_kernel(*refs, backward):
    if backward:
        (xs_ref, bm_ref, cm_ref, dt_ref, bias_ref, alog_ref, yf_ref, z_ref, dvec_ref, ng_ref,
         o_ref, h_ref) = refs
    else:
        xs_ref, bm_ref, cm_ref, dt_ref, bias_ref, alog_ref, o_ref, h_ref = refs

    @pl.when(pl.program_id(2) == 0)
    def _():
        h_ref[...] = jnp.zeros_like(h_ref)

    lane = lax.broadcasted_iota(jnp.int32, (CHUNK, LANE), 1)
    row = lax.broadcasted_iota(jnp.int32, (CHUNK, LANE), 0)
    hpg = HEADS_PER_GROUP

    dt = _softplus(dt_ref[...] + bias_ref[...])
    a = dt * (-jnp.exp(alog_ref[...]))
    tri_incl = lane <= row
    t_lo = jnp.where(tri_incl, 1.0, 0.0).astype(BF16)
    t_up = jnp.where(lane >= row, 1.0, 0.0).astype(BF16)
    cs = jnp.where(lane < hpg, _dot_exact_l(t_lo, a), _dot_exact_l(t_up, a))

    xs = xs_ref[...]
    bm = bm_ref[...]
    cm = cm_ref[...]
    xs_f = xs.astype(F32)

    erow = lax.broadcasted_iota(jnp.int32, (LANE, GROUP_W), 0)
    ecol = lax.broadcasted_iota(jnp.int32, (LANE, GROUP_W), 1)
    off = hpg if backward else 0
    expand = jnp.where(erow == (ecol // SSD_HEAD_DIM) + off, 1.0, 0.0).astype(BF16)

    if backward:
        edge = cs[0:1, :]
    else:
        edge = cs[CHUNK - 1:CHUNK, :]
    used = jnp.logical_and(lane >= off, lane < off + hpg)
    e_pos = jnp.where(used, jnp.exp(jnp.where(used, cs, 0.0)), 0.0)
    w_pos = jnp.where(used, jnp.exp(jnp.where(used, edge - cs, 0.0)) * dt, 0.0)
    ex = _dot_exact_r(jnp.concatenate([e_pos, w_pos], axis=0), expand)
    e_exp = ex[:CHUNK]
    w_exp = ex[CHUNK:]
    if backward:
        chunk_decay = e_exp[0:1, :]
    else:
        chunk_decay = e_exp[CHUNK - 1:CHUNK, :]

    h = h_ref[...]
    y_off = _dot(cm, h.astype(BF16)) * e_exp
    s_t = _dot_tn(bm, (xs_f * w_exp).astype(BF16))
    h_ref[...] = h * chunk_decay + s_t

    if not backward:
        cs_t = cs.T
        dt_t = dt.T
        cb = _dot_nt(cm, bm)
        pairs = []
        for pp in range(hpg // 2):
            x_pair = xs_f[:, pp * LANE:(pp + 1) * LANE]
            ms = []
            for hh in (2 * pp, 2 * pp + 1):
                arg = jnp.where(tri_incl,
                                cs[:, hh:hh + 1] - cs_t[hh:hh + 1, :],
                                cs[:, hpg + hh:hpg + hh + 1] - cs_t[hpg + hh:hpg + hh + 1, :])
                dts = jnp.where(tri_incl, dt_t[hh:hh + 1, :], dt_t[hpg + hh:hpg + hh + 1, :])
                ms.append((cb * jnp.exp(arg) * dts).astype(BF16))
            x_a = jnp.where(lane < SSD_HEAD_DIM, x_pair, 0.0).astype(BF16)
            x_b = jnp.where(lane >= SSD_HEAD_DIM, x_pair, 0.0).astype(BF16)
            pairs.append(_dot(jnp.concatenate(ms, axis=1), jnp.concatenate([x_a, x_b], axis=0)))
        o_ref[...] = jnp.concatenate(pairs, axis=1) + y_off
    else:
        y = yf_ref[...] + y_off + dvec_ref[...] * xs_f
        y = y * _silu(z_ref[...].astype(F32))
        y = y * lax.rsqrt(jnp.mean(y * y, axis=-1, keepdims=True) + RMS_EPS) * ng_ref[...]
        o_ref[...] = y.astype(BF16)


def _ssd(xbc3, dtp3, bias_g, alog_g, h3, dvec_g, ng_g):
    bsz, seq, _ = xbc3.shape
    nc = seq // CHUNK
    g_n = N_SSD_GROUPS
    b_blk0 = D_INNER // D_STATE
    c_blk0 = (D_INNER + BC_W) // D_STATE
    scratch = [pltpu.VMEM((D_STATE, GROUP_W), F32)]
    sem = ("parallel", "parallel", "arbitrary")

    def common_specs(cidx):
        return [
            pl.BlockSpec((None, CHUNK, GROUP_W), lambda b, g, j: (b, cidx(j), g)),
            pl.BlockSpec((None, CHUNK, D_STATE), lambda b, g, j: (b, cidx(j), b_blk0 + g)),
            pl.BlockSpec((None, CHUNK, D_STATE), lambda b, g, j: (b, cidx(j), c_blk0 + g)),
            pl.BlockSpec((None, CHUNK, LANE), lambda b, g, j: (b, cidx(j), g)),
            pl.BlockSpec((None, 1, LANE), lambda b, g, j: (g, 0, 0)),
            pl.BlockSpec((None, 1, LANE), lambda b, g, j: (g, 0, 0)),
        ]

    fwd = lambda j: j
    y_f = pl.pallas_call(
        functools.partial(_ssd_kernel, backward=False),
        grid=(bsz, g_n, nc),
        in_specs=common_specs(fwd),
        out_specs=pl.BlockSpec((None, CHUNK, GROUP_W), lambda b, g, j: (b, j, g)),
        out_shape=jax.ShapeDtypeStruct((bsz, seq, D_INNER), F32),
        scratch_shapes=scratch,
        compiler_params=_cparams(sem),
        name="ssd_fwd",
    )(xbc3, xbc3, xbc3, dtp3, bias_g, alog_g)

    bwd = lambda j: nc - 1 - j
    return pl.pallas_call(
        functools.partial(_ssd_kernel, backward=True),
        grid=(bsz, g_n, nc),
        in_specs=common_specs(bwd) + [
            pl.BlockSpec((None, CHUNK, GROUP_W), lambda b, g, j: (b, bwd(j), g)),
            pl.BlockSpec((None, CHUNK, GROUP_W), lambda b, g, j: (b, bwd(j), Z_BLK0 + g)),
            pl.BlockSpec((None, 1, GROUP_W), lambda b, g, j: (g, 0, 0)),
            pl.BlockSpec((None, 1, GROUP_W), lambda b, g, j: (g, 0, 0)),
        ],
        out_specs=pl.BlockSpec((None, CHUNK, GROUP_W), lambda b, g, j: (b, bwd(j), g)),
        out_shape=jax.ShapeDtypeStruct((bsz, seq, D_INNER), BF16),
        scratch_shapes=scratch,
        compiler_params=_cparams(sem),
        name="ssd_bwd",
    )(xbc3, xbc3, xbc3, dtp3, bias_g, alog_g, y_f, h3, dvec_g, ng_g)


def _merge_kernel(ya_ref, ys_ref, ga_ref, gs_ref, x_ref, wa_ref, ws_ref, wo_ref, g_ref, b_ref,
                  o_ref, ob_ref, acc_ref):
    n = pl.program_id(1)

    @pl.when(n == 0)
    def _():
        acc_ref[...] = jnp.zeros_like(acc_ref)

    merged = (ga_ref[...].astype(F32) * _dot(ya_ref[...], wa_ref[...])
              + gs_ref[...].astype(F32) * _dot(ys_ref[...], ws_ref[...]))
    acc_ref[...] += _dot(merged.astype(BF16), wo_ref[...])

    @pl.when(n == pl.num_programs(1) - 1)
    def _():
        o = _layer_norm(ALPHA * x_ref[...] + acc_ref[...], g_ref[...], b_ref[...])
        o_ref[...] = o
        ob_ref[...] = o.astype(BF16)


def _merge(y_attn, y_ssd, h, x1, wa, ws, wo, g, b, *, tm=512, tn=256):
    m, d = x1.shape
    n_blk = d // tn
    gate0 = GATE_BLK0 * COL_BLK // tn
    return pl.pallas_call(
        _merge_kernel,
        grid=(m // tm, n_blk),
        in_specs=[
            pl.BlockSpec((tm, Q_W), lambda i, n: (i, 0)),
            pl.BlockSpec((tm, D_INNER), lambda i, n: (i, 0)),
            pl.BlockSpec((tm, tn), lambda i, n: (i, gate0 + n)),
            pl.BlockSpec((tm, tn), lambda i, n: (i, gate0 + n_blk + n)),
            pl.BlockSpec((tm, d), lambda i, n: (i, 0)),
            pl.BlockSpec((Q_W, tn), lambda i, n: (0, n)),
            pl.BlockSpec((D_INNER, tn), lambda i, n: (0, n)),
            pl.BlockSpec((tn, d), lambda i, n: (n, 0)),
            pl.BlockSpec((1, d), lambda i, n: (0, 0)),
            pl.BlockSpec((1, d), lambda i, n: (0, 0)),
        ],
        out_specs=[pl.BlockSpec((tm, d), lambda i, n: (i, 0)),
                   pl.BlockSpec((tm, d), lambda i, n: (i, 0))],
        out_shape=[jax.ShapeDtypeStruct((m, d), F32), jax.ShapeDtypeStruct((m, d), BF16)],
        scratch_shapes=[pltpu.VMEM((tm, d), F32)],
        compiler_params=_cparams(("parallel", "arbitrary")),
        name="merge",
    )(y_attn, y_ssd, h, h, x1, wa, ws, wo, g, b)


def _cross_kernel(x_ref, xb_ref, wq_ref, kv_ref, wo_ref, g_ref, b_ref, o_ref, ob_ref):
    scale = MEM_HEAD_DIM ** -0.5
    q = (_dot(xb_ref[...], wq_ref[...]) * scale).astype(BF16)
    outs = []
    for hh in range(N_MEM_HEADS):
        sl = slice(hh * MEM_HEAD_DIM, (hh + 1) * MEM_HEAD_DIM)
        k = kv_ref[:, sl]
        v = kv_ref[:, MEM_W + hh * MEM_HEAD_DIM:MEM_W + (hh + 1) * MEM_HEAD_DIM]
        s = _dot_nt(q[:, sl], k)
        p = jnp.exp(s - jnp.max(s, axis=-1, keepdims=True))
        p = p / jnp.sum(p, axis=-1, keepdims=True)
        outs.append(_dot(p.astype(BF16), v))
    o_att = jnp.concatenate(outs, axis=1).astype(BF16)
    y = ALPHA * x_ref[...] + _dot(o_att, wo_ref[...])
    o = _layer_norm(y, g_ref[...], b_ref[...])
    o_ref[...] = o
    ob_ref[...] = o.astype(BF16)


def _cross(x2, x2b, wq, kv3, wo, g, b, *, seq, tm=512):
    m, d = x2.shape
    tm = min(tm, seq)
    t_per_seq = seq // tm
    mem_len = kv3.shape[1]
    return pl.pallas_call(
        _cross_kernel,
        grid=(m // tm,),
        in_specs=[
            pl.BlockSpec((tm, d), lambda i: (i, 0)),
            pl.BlockSpec((tm, d), lambda i: (i, 0)),
            pl.BlockSpec((d, MEM_W), lambda i: (0, 0)),
            pl.BlockSpec((None, mem_len, 2 * MEM_W), lambda i: (i // t_per_seq, 0, 0)),
            pl.BlockSpec((MEM_W, d), lambda i: (0, 0)),
            pl.BlockSpec((1, d), lambda i: (0, 0)),
            pl.BlockSpec((1, d), lambda i: (0, 0)),
        ],
        out_specs=[pl.BlockSpec((tm, d), lambda i: (i, 0)),
                   pl.BlockSpec((tm, d), lambda i: (i, 0))],
        out_shape=[jax.ShapeDtypeStruct((m, d), F32), jax.ShapeDtypeStruct((m, d), BF16)],
        compiler_params=_cparams(("parallel",)),
        name="cross_attn",
    )(x2, x2b, wq, kv3, wo, g, b)


def _rope_tables(seq):
    t = jnp.arange(seq)
    row = (t // GRID_W).astype(F32)
    col = (t % GRID_W).astype(F32)
    inv_freq = ROPE_THETA ** (-jnp.arange(ROPE_AXIS_DIM // 2, dtype=F32) * 2.0 / ROPE_AXIS_DIM)
    ang_r = row[:, None] * inv_freq[None, :]
    ang_c = col[:, None] * inv_freq[None, :]
    cr, sr, cc, sc = jnp.cos(ang_r), jnp.sin(ang_r), jnp.cos(ang_c), jnp.sin(ang_c)
    zr = jnp.zeros_like(sr)
    rc = jnp.concatenate([cr, cr, cc, cc], axis=1)
    ra = jnp.concatenate([-sr, zr, -sc, zr], axis=1)
    rb = jnp.concatenate([zr, sr, zr, sc], axis=1)
    return rc, ra, rb


def _group_lanes(v):
    g_n, hpg = N_SSD_GROUPS, HEADS_PER_GROUP
    both = jnp.concatenate([v[0].reshape(g_n, hpg), v[1].reshape(g_n, hpg)], axis=1)
    return jnp.pad(both, ((0, 0), (0, LANE - 2 * hpg))).reshape(g_n, 1, LANE)


def _dt_weight(w_dt):
    d = w_dt.shape[0]
    g_n, hpg = N_SSD_GROUPS, HEADS_PER_GROUP
    wf = w_dt[:, :N_SSD_HEADS].reshape(d, g_n, hpg)
    wb = w_dt[:, N_SSD_HEADS:].reshape(d, g_n, hpg)
    both = jnp.concatenate([wf, wb], axis=2)
    return jnp.pad(both, ((0, 0), (0, 0), (0, LANE - 2 * hpg))).reshape(d, g_n * LANE)


def kernel(x, mem, ffn1_w1, ffn1_w3, ffn1_w2, ln1_g, ln1_b, w_in, q_norm_g, k_norm_g, conv_w, conv_b,
           dt_bias, a_log, ssd_d, ssd_norm_g, w_branch_attn, w_branch_ssd, w_mix_out, ln2_g, ln2_b,
           mem_wq, mem_wk, mem_wv, mem_wo, ln3_g, ln3_b, ffn2_w1, ffn2_w3, ffn2_w2, ln4_g, ln4_b):
    bsz, seq, d = x.shape
    m = bsz * seq
    mem_len = mem.shape[1]
    bf = lambda t: t.astype(BF16)
    rc, ra, rb = _rope_tables(seq)
    xc = x.reshape(m, d)
    for l in range(DEPTH):
        row = lambda v: v[l].reshape(1, -1)
        x1, x1b = _ffn_ln(xc, bf(ffn1_w1[l]), bf(ffn1_w3[l]), bf(ffn1_w2[l]), row(ln1_g), row(ln1_b),
                          emit_bf16=True)
        dt0 = Q_W + 2 * KV_W + D_INNER + XBC_W
        w_main = bf(jnp.concatenate([w_in[l][:, :dt0], w_in[l][:, dt0 + DT_W:]], axis=1))
        gains = jnp.concatenate([jnp.tile(q_norm_g[l] * HEAD_DIM ** -0.5, N_Q_HEADS),
                                 jnp.tile(k_norm_g[l], N_KV_HEADS)]).reshape(1, -1)
        h = _in_proj(x1b, w_main, gains, rc, ra, rb, seq=seq)
        dtp = _matmul(x1b, bf(_dt_weight(w_in[l][:, dt0:dt0 + DT_W])), F32, tm=1024,
                      tn=N_SSD_GROUPS * LANE)
        h3 = h.reshape(bsz, seq, H_W)
        y_attn = _attention(h3)
        xbc = _conv_silu(h3, conv_w[l], conv_b[l].reshape(1, -1))
        y_ssd = _ssd(xbc, dtp.reshape(bsz, seq, N_SSD_GROUPS * LANE), _group_lanes(dt_bias[l]),
                     _group_lanes(a_log[l]), h3,
                     jnp.repeat(ssd_d[l], SSD_HEAD_DIM).reshape(N_SSD_GROUPS, 1, GROUP_W),
                     ssd_norm_g[l].reshape(N_SSD_GROUPS, 1, GROUP_W))
        x2, x2b = _merge(y_attn.reshape(m, Q_W), y_ssd.reshape(m, D_INNER), h, x1,
                         bf(w_branch_attn[l]), bf(w_branch_ssd[l]), bf(w_mix_out[l]),
                         row(ln2_g), row(ln2_b))
        w_kv = bf(jnp.concatenate([mem_wk[l], mem_wv[l]], axis=1))
        kv = _matmul(bf(mem.reshape(bsz * mem_len, d)), w_kv, BF16, tm=bsz * mem_len, tn=COL_BLK)
        x3, _ = _cross(x2, x2b, bf(mem_wq[l]), kv.reshape(bsz, mem_len, 2 * MEM_W), bf(mem_wo[l]),
                       row(ln3_g), row(ln3_b), seq=seq)
        (xc,) = _ffn_ln(x3, bf(ffn2_w1[l]), bf(ffn2_w3[l]), bf(ffn2_w2[l]), row(ln4_g), row(ln4_b),
                        emit_bf16=False)
    return xc.reshape(bsz, seq, d)
```

```python
import functools
import math

import jax
import jax.numpy as jnp
from jax import lax
from jax.experimental import pallas as pl
from jax.experimental.pallas import tpu as pltpu

F32 = jnp.float32
BF16 = jnp.bfloat16

D_MODEL = 2048
DEPTH = 1
GRID_W = 64
HEAD_DIM = 128
N_Q_HEADS = 16
N_KV_HEADS = 4
GQA_REP = 4
ROPE_AXIS_DIM = 64
ROPE_THETA = 10000.0
D_INNER = 4096
SSD_HEAD_DIM = 64
N_SSD_HEADS = 64
N_SSD_GROUPS = 8
HEADS_PER_GROUP = N_SSD_HEADS // N_SSD_GROUPS
GROUP_W = D_INNER // N_SSD_GROUPS
D_STATE = 128
CONV_W = 5
CHUNK = 128
N_MEM_HEADS = 4
MEM_HEAD_DIM = 128
D_FF = 5632
LN_EPS = 1e-5
RMS_EPS = 1e-6
Q_W = 2048
KV_W = 512
BC_W = 1024
XBC_W = 6144
DT_W = 128
MEM_W = 512
ALPHA = (2.0 * DEPTH) ** 0.25
LOG2E = math.log2(math.e)

LANE = 128
COL_BLK = 512
Z_BLK0 = 0
QKV_BLK0 = Z_BLK0 + D_INNER // COL_BLK
V_BLK = QKV_BLK0 + (Q_W + KV_W) // COL_BLK
XBC_BLK0 = V_BLK + 1
GATE_BLK0 = XBC_BLK0 + XBC_W // COL_BLK
H_BLKS = GATE_BLK0 + 2 * D_MODEL // COL_BLK
H_W = H_BLKS * COL_BLK

VMEM_LIMIT = 56 * 1024 * 1024


def _cparams(sem):
    return pltpu.CompilerParams(dimension_semantics=sem, vmem_limit_bytes=VMEM_LIMIT)


def _dot(a, b):
    return jnp.dot(a, b, preferred_element_type=F32)


def _dot_nt(a, b):
    return lax.dot_general(a, b, (((1,), (1,)), ((), ())), preferred_element_type=F32)


def _dot_tn(a, b):
    return lax.dot_general(a, b, (((0,), (0,)), ((), ())), preferred_element_type=F32)


def _split3(v):
    hi = v.astype(BF16)
    r1 = v - hi.astype(F32)
    mid = r1.astype(BF16)
    lo = (r1 - mid.astype(F32)).astype(BF16)
    return hi, mid, lo


def _dot_exact_r(v, m01):
    hi, mid, lo = _split3(v)
    return _dot(hi, m01) + _dot(mid, m01) + _dot(lo, m01)


def _dot_exact_l(m01, v):
    hi, mid, lo = _split3(v)
    return _dot(m01, hi) + _dot(m01, mid) + _dot(m01, lo)


def _layer_norm(y, g, b):
    mu = jnp.mean(y, axis=-1, keepdims=True)
    d = y - mu
    var = jnp.mean(d * d, axis=-1, keepdims=True)
    return d * lax.rsqrt(var + LN_EPS) * g + b


def _silu(x):
    return x * jax.nn.sigmoid(x)


def _ffn_ln_kernel(x_ref, w1_ref, w3_ref, w2_ref, g_ref, b_ref, *rest, emit_bf16):
    if emit_bf16:
        o_ref, ob_ref, xb_ref, acc_ref = rest
    else:
        o_ref, xb_ref, acc_ref = rest
        ob_ref = None
    f = pl.program_id(1)

    @pl.when(f == 0)
    def _():
        xb_ref[...] = x_ref[...].astype(BF16)
        acc_ref[...] = jnp.zeros_like(acc_ref)

    xb = xb_ref[...]
    h1 = _dot(xb, w1_ref[...])
    h3 = _dot(xb, w3_ref[...])
    a = (_silu(h1) * h3).astype(BF16)
    acc_ref[...] += _dot(a, w2_ref[...])

    @pl.when(f == pl.num_programs(1) - 1)
    def _():
        y = ALPHA * x_ref[...] + 0.5 * acc_ref[...]
        o = _layer_norm(y, g_ref[...], b_ref[...])
        o_ref[...] = o
        if ob_ref is not None:
            ob_ref[...] = o.astype(BF16)


def _ffn_ln(x, w1, w3, w2, g, b, *, emit_bf16, tm=512, tf=512):
    m, d = x.shape
    dff = w1.shape[1]
    out_shape = [jax.ShapeDtypeStruct((m, d), F32)]
    out_specs = [pl.BlockSpec((tm, d), lambda i, f: (i, 0))]
    if emit_bf16:
        out_shape.append(jax.ShapeDtypeStruct((m, d), BF16))
        out_specs.append(pl.BlockSpec((tm, d), lambda i, f: (i, 0)))
    return pl.pallas_call(
        functools.partial(_ffn_ln_kernel, emit_bf16=emit_bf16),
        grid=(m // tm, dff // tf),
        in_specs=[
            pl.BlockSpec((tm, d), lambda i, f: (i, 0)),
            pl.BlockSpec((d, tf), lambda i, f: (0, f)),
            pl.BlockSpec((d, tf), lambda i, f: (0, f)),
            pl.BlockSpec((tf, d), lambda i, f: (f, 0)),
            pl.BlockSpec((1, d), lambda i, f: (0, 0)),
            pl.BlockSpec((1, d), lambda i, f: (0, 0)),
        ],
        out_specs=out_specs,
        out_shape=out_shape,
        scratch_shapes=[pltpu.VMEM((tm, d), BF16), pltpu.VMEM((tm, d), F32)],
        compiler_params=_cparams(("parallel", "arbitrary")),
        name="ffn_ln",
    )(x, w1, w3, w2, g, b)


def _in_proj_kernel(x_ref, w_ref, gain_ref, rc_ref, ra_ref, rb_ref, o_ref):
    n = pl.program_id(1)
    acc = _dot(x_ref[...], w_ref[...])

    @pl.when(jnp.logical_and(n >= QKV_BLK0, n < V_BLK))
    def _():
        rc = rc_ref[...]
        ra = ra_ref[...]
        rb = rb_ref[...]
        for hh in range(COL_BLK // HEAD_DIM):
            sl = slice(hh * HEAD_DIM, (hh + 1) * HEAD_DIM)
            xh = acc[:, sl]
            ms = jnp.mean(xh * xh, axis=-1, keepdims=True)
            xh = xh * lax.rsqrt(ms + RMS_EPS) * gain_ref[:, sl]
            r = xh * rc + pltpu.roll(xh, 96, 1) * ra + pltpu.roll(xh, 32, 1) * rb
            o_ref[:, sl] = r.astype(BF16)

    @pl.when(jnp.logical_or(n < QKV_BLK0, jnp.logical_and(n >= V_BLK, n < GATE_BLK0)))
    def _():
        o_ref[...] = acc.astype(BF16)

    @pl.when(n >= GATE_BLK0)
    def _():
        o_ref[...] = jax.nn.sigmoid(acc).astype(BF16)


def _in_proj(xb, w, gains, rc, ra, rb, *, seq, tm=1024):
    m, d = xb.shape
    tm = min(tm, seq)
    t_per_seq = seq // tm
    n_gain_blk = gains.shape[1] // COL_BLK
    return pl.pallas_call(
        _in_proj_kernel,
        grid=(m // tm, H_BLKS),
        in_specs=[
            pl.BlockSpec((tm, d), lambda i, n: (i, 0)),
            pl.BlockSpec((d, COL_BLK), lambda i, n: (0, n)),
            pl.BlockSpec((1, COL_BLK), lambda i, n: (0, jnp.clip(n - QKV_BLK0, 0, n_gain_blk - 1))),
            pl.BlockSpec((tm, HEAD_DIM), lambda i, n: (i % t_per_seq, 0)),
            pl.BlockSpec((tm, HEAD_DIM), lambda i, n: (i % t_per_seq, 0)),
            pl.BlockSpec((tm, HEAD_DIM), lambda i, n: (i % t_per_seq, 0)),
        ],
        out_specs=pl.BlockSpec((tm, COL_BLK), lambda i, n: (i, n)),
        out_shape=jax.ShapeDtypeStruct((m, H_W), BF16),
        compiler_params=_cparams(("parallel", "arbitrary")),
        name="in_proj",
    )(xb, w, gains, rc, ra, rb)


def _matmul_kernel(x_ref, w_ref, o_ref):
    o_ref[...] = _dot(x_ref[...], w_ref[...]).astype(o_ref.dtype)


def _matmul(xb, w, out_dtype, *, tm, tn):
    m, d = xb.shape
    n = w.shape[1]
    tm = min(tm, m)
    return pl.pallas_call(
        _matmul_kernel,
        grid=(m // tm, n // tn),
        in_specs=[pl.BlockSpec((tm, d), lambda i, j: (i, 0)),
                  pl.BlockSpec((d, tn), lambda i, j: (0, j))],
        out_specs=pl.BlockSpec((tm, tn), lambda i, j: (i, j)),
        out_shape=jax.ShapeDtypeStruct((m, n), out_dtype),
        compiler_params=_cparams(("parallel", "arbitrary")),
        name="matmul",
    )(xb, w)


def _attn_kernel(q_ref, k_ref, v_ref, o_ref, v1_ref, *, tq, tk):
    seq = k_ref.shape[0]

    @pl.when(pl.program_id(2) == 0)
    def _():
        v1_ref[:, :HEAD_DIM] = v_ref[...]
        v1_ref[:, HEAD_DIM:] = jnp.ones((seq, HEAD_DIM), BF16)

    q = jnp.concatenate([q_ref[:, hh * HEAD_DIM:(hh + 1) * HEAD_DIM] for hh in range(GQA_REP)], axis=0)
    rows = GQA_REP * tq
    m_i = jnp.full((rows, 1), -jnp.inf, F32)
    acc = jnp.zeros((rows, 2 * HEAD_DIM), F32)
    for j in range(seq // tk):
        s = _dot_nt(q, k_ref[j * tk:(j + 1) * tk, :])
        m_new = jnp.maximum(m_i, jnp.max(s, axis=-1, keepdims=True))
        p = jnp.exp2(s - m_new).astype(BF16)
        acc = jnp.exp2(m_i - m_new) * acc + _dot(p, v1_ref[j * tk:(j + 1) * tk, :])
        m_i = m_new
    o = (acc[:, :HEAD_DIM] / acc[:, HEAD_DIM:HEAD_DIM + 1]).astype(BF16)
    for hh in range(GQA_REP):
        o_ref[:, hh * HEAD_DIM:(hh + 1) * HEAD_DIM] = o[hh * tq:(hh + 1) * tq]


def _attention(h3, *, tq=128, tk=512):
    bsz, seq, _ = h3.shape
    tk = min(tk, seq)
    k_blk0 = (QKV_BLK0 * COL_BLK + Q_W) // HEAD_DIM
    v_blk0 = V_BLK * COL_BLK // HEAD_DIM
    return pl.pallas_call(
        functools.partial(_attn_kernel, tq=tq, tk=tk),
        grid=(bsz, N_KV_HEADS, seq // tq),
        in_specs=[
            pl.BlockSpec((None, tq, GQA_REP * HEAD_DIM), lambda b, kv, i: (b, i, QKV_BLK0 + kv)),
            pl.BlockSpec((None, seq, HEAD_DIM), lambda b, kv, i: (b, 0, k_blk0 + kv)),
            pl.BlockSpec((None, seq, HEAD_DIM), lambda b, kv, i: (b, 0, v_blk0 + kv)),
        ],
        out_specs=pl.BlockSpec((None, tq, GQA_REP * HEAD_DIM), lambda b, kv, i: (b, i, kv)),
        out_shape=jax.ShapeDtypeStruct((bsz, seq, Q_W), BF16),
        scratch_shapes=[pltpu.VMEM((seq, 2 * HEAD_DIM), BF16)],
        compiler_params=_cparams(("parallel", "parallel", "arbitrary")),
        name="attention",
    )(h3, h3, h3)


CONV_HALO = 8
CONV_SUB = 64


def _conv_kernel(prev_ref, main_ref, next_ref, w_ref, b_ref, o_ref, win_ref, *, rt):
    r = pl.program_id(1)
    first = r == 0
    last = r == pl.num_programs(1) - 1
    win_ref[0:CONV_HALO, :] = jnp.where(first, 0.0, prev_ref[...].astype(F32))
    win_ref[CONV_HALO:CONV_HALO + rt, :] = main_ref[...].astype(F32)
    win_ref[CONV_HALO + rt:, :] = jnp.where(last, 0.0, next_ref[...].astype(F32))
    pad = CONV_W // 2
    bias = b_ref[...]
    for t in range(rt // CONV_SUB):
        base = CONV_HALO + t * CONV_SUB - pad
        acc = bias + w_ref[0:1, :] * win_ref[base:base + CONV_SUB, :]
        for k in range(1, CONV_W):
            acc = acc + w_ref[k:k + 1, :] * win_ref[base + k:base + k + CONV_SUB, :]
        o_ref[t * CONV_SUB:(t + 1) * CONV_SUB, :] = _silu(acc).astype(BF16)


def _conv_silu(h3, conv_w, conv_b, *, rt=512):
    bsz, seq, _ = h3.shape
    rt = min(rt, seq)
    nh = rt // CONV_HALO
    n_halo_blk = seq // CONV_HALO
    return pl.pallas_call(
        functools.partial(_conv_kernel, rt=rt),
        grid=(bsz, seq // rt, XBC_W // COL_BLK),
        in_specs=[
            pl.BlockSpec((None, CONV_HALO, COL_BLK),
                         lambda b, r, c: (b, jnp.maximum(r * nh - 1, 0), XBC_BLK0 + c)),
            pl.BlockSpec((None, rt, COL_BLK), lambda b, r, c: (b, r, XBC_BLK0 + c)),
            pl.BlockSpec((None, CONV_HALO, COL_BLK),
                         lambda b, r, c: (b, jnp.minimum((r + 1) * nh, n_halo_blk - 1), XBC_BLK0 + c)),
            pl.BlockSpec((CONV_W, COL_BLK), lambda b, r, c: (0, c)),
            pl.BlockSpec((1, COL_BLK), lambda b, r, c: (0, c)),
        ],
        out_specs=pl.BlockSpec((None, rt, COL_BLK), lambda b, r, c: (b, r, c)),
        out_shape=jax.ShapeDtypeStruct((bsz, seq, XBC_W), BF16),
        scratch_shapes=[pltpu.VMEM((rt + 2 * CONV_HALO, COL_BLK), F32)],
        compiler_params=_cparams(("parallel", "parallel", "arbitrary")),
        name="conv_silu",
    )(h3, h3, h3, conv_w, conv_b)


def _softplus(x):
    return jnp.maximum(x, 0.0) + jnp.log1p(jnp.exp(-jnp.abs(x)))


def _cumsum_pos(t01, a):
    hi, mid, lo = _split3(a)
    t2 = jnp.concatenate([t01, t01], axis=1)
    return _dot(t2, jnp.concatenate([hi, mid], axis=0)) + _dot(t01, lo)


def _ssd_kernel(*refs, backward):
    if backward:
        (xs_ref, bm_ref, cm_ref, dt_ref, bias_ref, alog_ref, ex_ref, yf_ref, z_ref, dvec_ref, ng_ref,
         o_ref, h_ref) = refs
    else:
        xs_ref, bm_ref, cm_ref, dt_ref, bias_ref, alog_ref, ex_ref, o_ref, h_ref = refs

    @pl.when(pl.program_id(1) == 0)
    def _():
        h_ref[...] = jnp.zeros_like(h_ref)

    lane = lax.broadcasted_iota(jnp.int32, (CHUNK, LANE), 1)
    row = lax.broadcasted_iota(jnp.int32, (CHUNK, LANE), 0)
    nh = N_SSD_HEADS
    hpg = HEADS_PER_GROUP

    dt = _softplus(dt_ref[...] + bias_ref[...])
    a = dt * (-jnp.exp(alog_ref[...]))
    tri_incl = lane <= row
    t_lo = jnp.where(tri_incl, 1.0, 0.0).astype(BF16)
    t_up = jnp.where(lane >= row, 1.0, 0.0).astype(BF16)
    cs = jnp.where(lane < nh, _cumsum_pos(t_lo, a), _cumsum_pos(t_up, a))

    off = nh if backward else 0
    if backward:
        edge = cs[0:1, :]
    else:
        edge = cs[CHUNK - 1:CHUNK, :]
    used = jnp.logical_and(lane >= off, lane < off + nh)
    e_pos = jnp.where(used, jnp.exp(jnp.where(used, cs, 0.0)), 0.0)
    w_pos = jnp.where(used, jnp.exp(jnp.where(used, edge - cs, 0.0)) * dt, 0.0)
    st = jnp.concatenate([e_pos, w_pos], axis=0)
    st_hi = st.astype(BF16)
    st_mid = (st - st_hi.astype(F32)).astype(BF16)
    st2 = jnp.concatenate([st_hi, st_mid], axis=1)
    if not backward:
        cs_t = cs.T
        dt_t = dt.T

    for g in range(N_SSD_GROUPS):
        sl = slice(g * GROUP_W, (g + 1) * GROUP_W)
        sb = slice(g * D_STATE, (g + 1) * D_STATE)
        xs_f = xs_ref[:, sl].astype(F32)
        bm = bm_ref[:, sb]
        cm = cm_ref[:, sb]
        ex = _dot(st2, ex_ref[g])
        e_exp = ex[:CHUNK]
        w_exp = ex[CHUNK:]
        chunk_decay = e_exp[0:1, :] if backward else e_exp[CHUNK - 1:CHUNK, :]
        h = h_ref[g]
        y_off = _dot(cm, h.astype(BF16)) * e_exp
        s_t = _dot_tn(bm, (xs_f * w_exp).astype(BF16))
        h_ref[g] = h * chunk_decay + s_t

        if not backward:
            cb = _dot_nt(cm, bm)
            pairs = []
            for pp in range(hpg // 2):
                x_pair = xs_f[:, pp * LANE:(pp + 1) * LANE]
                ms = []
                for hh in (g * hpg + 2 * pp, g * hpg + 2 * pp + 1):
                    arg = jnp.where(tri_incl,
                                    cs[:, hh:hh + 1] - cs_t[hh:hh + 1, :],
                                    cs[:, nh + hh:nh + hh + 1] - cs_t[nh + hh:nh + hh + 1, :])
                    dts = jnp.where(tri_incl, dt_t[hh:hh + 1, :], dt_t[nh + hh:nh + hh + 1, :])
                    ms.append((cb * jnp.exp(arg) * dts).astype(BF16))
                x_a = jnp.where(lane < SSD_HEAD_DIM, x_pair, 0.0).astype(BF16)
                x_b = jnp.where(lane >= SSD_HEAD_DIM, x_pair, 0.0).astype(BF16)
                pairs.append(_dot(jnp.concatenate(ms, axis=1), jnp.concatenate([x_a, x_b], axis=0)))
            o_ref[:, sl] = jnp.concatenate(pairs, axis=1) + y_off
        else:
            y = yf_ref[:, sl] + y_off + dvec_ref[:, sl] * xs_f
            y = y * _silu(z_ref[:, sl].astype(F32))
            y = y * lax.rsqrt(jnp.mean(y * y, axis=-1, keepdims=True) + RMS_EPS) * ng_ref[:, sl]
            o_ref[:, sl] = y.astype(BF16)


def _expand_mats(off):
    k = (jnp.arange(2 * LANE) % LANE)[None, :, None]
    col = (jnp.arange(GROUP_W) // SSD_HEAD_DIM)[None, None, :]
    g = jnp.arange(N_SSD_GROUPS)[:, None, None]
    return (k == off + HEADS_PER_GROUP * g + col).astype(BF16)


def _ssd(xbc3, dt3, bias, alog, h3, dvec, ng):
    bsz, seq, _ = xbc3.shape
    nc = seq // CHUNK
    g_n = N_SSD_GROUPS
    scratch = [pltpu.VMEM((g_n, D_STATE, GROUP_W), F32)]
    sem = ("parallel", "arbitrary")

    def common_specs(cidx):
        return [
            pl.BlockSpec((None, CHUNK, D_INNER), lambda b, j: (b, cidx(j), 0)),
            pl.BlockSpec((None, CHUNK, BC_W), lambda b, j: (b, cidx(j), D_INNER // BC_W)),
            pl.BlockSpec((None, CHUNK, BC_W), lambda b, j: (b, cidx(j), D_INNER // BC_W + 1)),
            pl.BlockSpec((None, CHUNK, LANE), lambda b, j: (b, cidx(j), 0)),
            pl.BlockSpec((1, LANE), lambda b, j: (0, 0)),
            pl.BlockSpec((1, LANE), lambda b, j: (0, 0)),
            pl.BlockSpec((g_n, 2 * LANE, GROUP_W), lambda b, j: (0, 0, 0)),
        ]

    fwd = lambda j: j
    y_f = pl.pallas_call(
        functools.partial(_ssd_kernel, backward=False),
        grid=(bsz, nc),
        in_specs=common_specs(fwd),
        out_specs=pl.BlockSpec((None, CHUNK, D_INNER), lambda b, j: (b, j, 0)),
        out_shape=jax.ShapeDtypeStruct((bsz, seq, D_INNER), F32),
        scratch_shapes=scratch,
        compiler_params=_cparams(sem),
        name="ssd_fwd",
    )(xbc3, xbc3, xbc3, dt3, bias, alog, _expand_mats(0))

    bwd = lambda j: nc - 1 - j
    return pl.pallas_call(
        functools.partial(_ssd_kernel, backward=True),
        grid=(bsz, nc),
        in_specs=common_specs(bwd) + [
            pl.BlockSpec((None, CHUNK, D_INNER), lambda b, j: (b, bwd(j), 0)),
            pl.BlockSpec((None, CHUNK, D_INNER), lambda b, j: (b, bwd(j), Z_BLK0 * COL_BLK // D_INNER)),
            pl.BlockSpec((1, D_INNER), lambda b, j: (0, 0)),
            pl.BlockSpec((1, D_INNER), lambda b, j: (0, 0)),
        ],
        out_specs=pl.BlockSpec((None, CHUNK, D_INNER), lambda b, j: (b, bwd(j), 0)),
        out_shape=jax.ShapeDtypeStruct((bsz, seq, D_INNER), BF16),
        scratch_shapes=scratch,
        compiler_params=_cparams(sem),
        name="ssd_bwd",
    )(xbc3, xbc3, xbc3, dt3, bias, alog, _expand_mats(N_SSD_HEADS), y_f, h3, dvec, ng)


def _merge_kernel(ya_ref, ys_ref, ga_ref, gs_ref, x_ref, wa_ref, ws_ref, wo_ref, g_ref, b_ref,
                  o_ref, ob_ref, acc_ref):
    n = pl.program_id(1)

    @pl.when(n == 0)
    def _():
        acc_ref[...] = jnp.zeros_like(acc_ref)

    merged = (ga_ref[...].astype(F32) * _dot(ya_ref[...], wa_ref[...])
              + gs_ref[...].astype(F32) * _dot(ys_ref[...], ws_ref[...]))
    acc_ref[...] += _dot(merged.astype(BF16), wo_ref[...])

    @pl.when(n == pl.num_programs(1) - 1)
    def _():
        o = _layer_norm(ALPHA * x_ref[...] + acc_ref[...], g_ref[...], b_ref[...])
        o_ref[...] = o
        ob_ref[...] = o.astype(BF16)


def _merge(y_attn, y_ssd, h, x1, wa, ws, wo, g, b, *, tm=512, tn=256):
    m, d = x1.shape
    n_blk = d // tn
    gate0 = GATE_BLK0 * COL_BLK // tn
    return pl.pallas_call(
        _merge_kernel,
        grid=(m // tm, n_blk),
        in_specs=[
            pl.BlockSpec((tm, Q_W), lambda i, n: (i, 0)),
            pl.BlockSpec((tm, D_INNER), lambda i, n: (i, 0)),
            pl.BlockSpec((tm, tn), lambda i, n: (i, gate0 + n)),
            pl.BlockSpec((tm, tn), lambda i, n: (i, gate0 + n_blk + n)),
            pl.BlockSpec((tm, d), lambda i, n: (i, 0)),
            pl.BlockSpec((Q_W, tn), lambda i, n: (0, n)),
            pl.BlockSpec((D_INNER, tn), lambda i, n: (0, n)),
            pl.BlockSpec((tn, d), lambda i, n: (n, 0)),
            pl.BlockSpec((1, d), lambda i, n: (0, 0)),
            pl.BlockSpec((1, d), lambda i, n: (0, 0)),
        ],
        out_specs=[pl.BlockSpec((tm, d), lambda i, n: (i, 0)),
                   pl.BlockSpec((tm, d), lambda i, n: (i, 0))],
        out_shape=[jax.ShapeDtypeStruct((m, d), F32), jax.ShapeDtypeStruct((m, d), BF16)],
        scratch_shapes=[pltpu.VMEM((tm, d), F32)],
        compiler_params=_cparams(("parallel", "arbitrary")),
        name="merge",
    )(y_attn, y_ssd, h, h, x1, wa, ws, wo, g, b)


def _cross_kernel(x_ref, xb_ref, wq_ref, kv_ref, wo_ref, g_ref, b_ref, o_ref, ob_ref):
    scale = MEM_HEAD_DIM ** -0.5
    q = (_dot(xb_ref[...], wq_ref[...]) * scale).astype(BF16)
    outs = []
    for hh in range(N_MEM_HEADS):
        sl = slice(hh * MEM_HEAD_DIM, (hh + 1) * MEM_HEAD_DIM)
        k = kv_ref[:, sl]
        v = kv_ref[:, MEM_W + hh * MEM_HEAD_DIM:MEM_W + (hh + 1) * MEM_HEAD_DIM]
        s = _dot_nt(q[:, sl], k)
        p = jnp.exp(s - jnp.max(s, axis=-1, keepdims=True))
        p = p / jnp.sum(p, axis=-1, keepdims=True)
        outs.append(_dot(p.astype(BF16), v))
    o_att = jnp.concatenate(outs, axis=1).astype(BF16)
    y = ALPHA * x_ref[...] + _dot(o_att, wo_ref[...])
    o = _layer_norm(y, g_ref[...], b_ref[...])
    o_ref[...] = o
    ob_ref[...] = o.astype(BF16)


def _cross(x2, x2b, wq, kv3, wo, g, b, *, seq, tm=512):
    m, d = x2.shape
    tm = min(tm, seq)
    t_per_seq = seq // tm
    mem_len = kv3.shape[1]
    return pl.pallas_call(
        _cross_kernel,
        grid=(m // tm,),
        in_specs=[
            pl.BlockSpec((tm, d), lambda i: (i, 0)),
            pl.BlockSpec((tm, d), lambda i: (i, 0)),
            pl.BlockSpec((d, MEM_W), lambda i: (0, 0)),
            pl.BlockSpec((None, mem_len, 2 * MEM_W), lambda i: (i // t_per_seq, 0, 0)),
            pl.BlockSpec((MEM_W, d), lambda i: (0, 0)),
            pl.BlockSpec((1, d), lambda i: (0, 0)),
            pl.BlockSpec((1, d), lambda i: (0, 0)),
        ],
        out_specs=[pl.BlockSpec((tm, d), lambda i: (i, 0)),
                   pl.BlockSpec((tm, d), lambda i: (i, 0))],
        out_shape=[jax.ShapeDtypeStruct((m, d), F32), jax.ShapeDtypeStruct((m, d), BF16)],
        compiler_params=_cparams(("parallel",)),
        name="cross_attn",
    )(x2, x2b, wq, kv3, wo, g, b)


def _rope_tables(seq):
    t = jnp.arange(seq)
    row = (t // GRID_W).astype(F32)
    col = (t % GRID_W).astype(F32)
    inv_freq = ROPE_THETA ** (-jnp.arange(ROPE_AXIS_DIM // 2, dtype=F32) * 2.0 / ROPE_AXIS_DIM)
    ang_r = row[:, None] * inv_freq[None, :]
    ang_c = col[:, None] * inv_freq[None, :]
    cr, sr, cc, sc = jnp.cos(ang_r), jnp.sin(ang_r), jnp.cos(ang_c), jnp.sin(ang_c)
    zr = jnp.zeros_like(sr)
    rc = jnp.concatenate([cr, cr, cc, cc], axis=1)
    ra = jnp.concatenate([-sr, zr, -sc, zr], axis=1)
    rb = jnp.concatenate([zr, sr, zr, sc], axis=1)
    return rc, ra, rb


def kernel(x, mem, ffn1_w1, ffn1_w3, ffn1_w2, ln1_g, ln1_b, w_in, q_norm_g, k_norm_g, conv_w, conv_b,
           dt_bias, a_log, ssd_d, ssd_norm_g, w_branch_attn, w_branch_ssd, w_mix_out, ln2_g, ln2_b,
           mem_wq, mem_wk, mem_wv, mem_wo, ln3_g, ln3_b, ffn2_w1, ffn2_w3, ffn2_w2, ln4_g, ln4_b):
    bsz, seq, d = x.shape
    m = bsz * seq
    mem_len = mem.shape[1]
    bf = lambda t: t.astype(BF16)
    rc, ra, rb = _rope_tables(seq)
    xc = x.reshape(m, d)
    for l in range(DEPTH):
        row = lambda v: v[l].reshape(1, -1)
        x1, x1b = _ffn_ln(xc, bf(ffn1_w1[l]), bf(ffn1_w3[l]), bf(ffn1_w2[l]), row(ln1_g), row(ln1_b),
                          emit_bf16=True)
        z0 = Q_W + 2 * KV_W
        xbc0 = z0 + D_INNER
        dt0 = xbc0 + XBC_W
        w_main = bf(jnp.concatenate([w_in[l][:, z0:xbc0], w_in[l][:, :z0], w_in[l][:, xbc0:dt0],
                                     w_in[l][:, dt0 + DT_W:]], axis=1))
        gains = jnp.concatenate([jnp.tile(q_norm_g[l] * (HEAD_DIM ** -0.5 * LOG2E), N_Q_HEADS),
                                 jnp.tile(k_norm_g[l], N_KV_HEADS)]).reshape(1, -1)
        h = _in_proj(x1b, w_main, gains, rc, ra, rb, seq=seq)
        dt_raw = _matmul(x1b, bf(w_in[l][:, dt0:dt0 + DT_W]), F32, tm=1024, tn=DT_W)
        h3 = h.reshape(bsz, seq, H_W)
        y_attn = _attention(h3)
        xbc = _conv_silu(h3, conv_w[l], conv_b[l].reshape(1, -1))
        y_ssd = _ssd(xbc, dt_raw.reshape(bsz, seq, DT_W), dt_bias[l].reshape(1, DT_W),
                     a_log[l].reshape(1, DT_W), h3,
                     jnp.repeat(ssd_d[l], SSD_HEAD_DIM).reshape(1, D_INNER),
                     ssd_norm_g[l].reshape(1, D_INNER))
        x2, x2b = _merge(y_attn.reshape(m, Q_W), y_ssd.reshape(m, D_INNER), h, x1,
                         bf(w_branch_attn[l]), bf(w_branch_ssd[l]), bf(w_mix_out[l]),
                         row(ln2_g), row(ln2_b))
        w_kv = bf(jnp.concatenate([mem_wk[l], mem_wv[l]], axis=1))
        kv = _matmul(bf(mem.reshape(bsz * mem_len, d)), w_kv, BF16, tm=bsz * mem_len, tn=COL_BLK)
        x3, _ = _cross(x2, x2b, bf(mem_wq[l]), kv.reshape(bsz, mem_len, 2 * MEM_W), bf(mem_wo[l]),
                       row(ln3_g), row(ln3_b), seq=seq)
        (xc,) = _ffn_ln(x3, bf(ffn2_w1[l]), bf(ffn2_w3[l]), bf(ffn2_w2[l]), row(ln4_g), row(ln4_b),
                        emit_bf16=False)
    return xc.reshape(bsz, seq, d)
```

```python
import functools
import math

import jax
import jax.numpy as jnp
from jax import lax
from jax.experimental import pallas as pl
from jax.experimental.pallas import tpu as pltpu

F32 = jnp.float32
BF16 = jnp.bfloat16

D_MODEL = 2048
DEPTH = 1
GRID_W = 64
HEAD_DIM = 128
N_Q_HEADS = 16
N_KV_HEADS = 4
GQA_REP = 4
ROPE_AXIS_DIM = 64
ROPE_THETA = 10000.0
D_INNER = 4096
SSD_HEAD_DIM = 64
N_SSD_HEADS = 64
N_SSD_GROUPS = 8
HEADS_PER_GROUP = N_SSD_HEADS // N_SSD_GROUPS
GROUP_W = D_INNER // N_SSD_GROUPS
D_STATE = 128
CONV_W = 5
CHUNK = 128
N_MEM_HEADS = 4
MEM_HEAD_DIM = 128
D_FF = 5632
LN_EPS = 1e-5
RMS_EPS = 1e-6
Q_W = 2048
KV_W = 512
BC_W = 1024
XBC_W = 6144
DT_W = 128
MEM_W = 512
ALPHA = (2.0 * DEPTH) ** 0.25
LOG2E = math.log2(math.e)

LANE = 128
COL_BLK = 512
Z_BLK0 = 0
XBC_BLK0 = Z_BLK0 + D_INNER // COL_BLK
QKV_BLK0 = XBC_BLK0 + XBC_W // COL_BLK
V_BLK = QKV_BLK0 + (Q_W + KV_W) // COL_BLK
GATE_BLK0 = V_BLK + 1
H_BLKS = GATE_BLK0 + 2 * D_MODEL // COL_BLK
H_W = H_BLKS * COL_BLK

VMEM_LIMIT = 56 * 1024 * 1024


def _cparams(sem):
    return pltpu.CompilerParams(dimension_semantics=sem, vmem_limit_bytes=VMEM_LIMIT)


def _dot(a, b):
    return jnp.dot(a, b, preferred_element_type=F32)


def _dot_nt(a, b):
    return lax.dot_general(a, b, (((1,), (1,)), ((), ())), preferred_element_type=F32)


def _dot_tn(a, b):
    return lax.dot_general(a, b, (((0,), (0,)), ((), ())), preferred_element_type=F32)


def _split3(v):
    hi = v.astype(BF16)
    r1 = v - hi.astype(F32)
    mid = r1.astype(BF16)
    lo = (r1 - mid.astype(F32)).astype(BF16)
    return hi, mid, lo


def _layer_norm(y, g, b):
    mu = jnp.mean(y, axis=-1, keepdims=True)
    d = y - mu
    var = jnp.mean(d * d, axis=-1, keepdims=True)
    return d * lax.rsqrt(var + LN_EPS) * g + b


def _silu(x):
    return x * jax.nn.sigmoid(x)


def _ffn_ln_kernel(x_ref, w1_ref, w3_ref, w2_ref, g_ref, b_ref, *rest, emit_bf16):
    if emit_bf16:
        o_ref, ob_ref, xb_ref, acc_ref = rest
    else:
        o_ref, xb_ref, acc_ref = rest
        ob_ref = None
    f = pl.program_id(1)

    @pl.when(f == 0)
    def _():
        xb_ref[...] = x_ref[...].astype(BF16)
        acc_ref[...] = jnp.zeros_like(acc_ref)

    xb = xb_ref[...]
    h1 = _dot(xb, w1_ref[...])
    h3 = _dot(xb, w3_ref[...])
    a = (_silu(h1) * h3).astype(BF16)
    acc_ref[...] += _dot(a, w2_ref[...])

    @pl.when(f == pl.num_programs(1) - 1)
    def _():
        y = ALPHA * x_ref[...] + 0.5 * acc_ref[...]
        o = _layer_norm(y, g_ref[...], b_ref[...])
        o_ref[...] = o
        if ob_ref is not None:
            ob_ref[...] = o.astype(BF16)


def _ffn_ln(x, w1, w3, w2, g, b, *, emit_bf16, tm=512, tf=512):
    m, d = x.shape
    dff = w1.shape[1]
    out_shape = [jax.ShapeDtypeStruct((m, d), F32)]
    out_specs = [pl.BlockSpec((tm, d), lambda i, f: (i, 0))]
    if emit_bf16:
        out_shape.append(jax.ShapeDtypeStruct((m, d), BF16))
        out_specs.append(pl.BlockSpec((tm, d), lambda i, f: (i, 0)))
    return pl.pallas_call(
        functools.partial(_ffn_ln_kernel, emit_bf16=emit_bf16),
        grid=(m // tm, dff // tf),
        in_specs=[
            pl.BlockSpec((tm, d), lambda i, f: (i, 0)),
            pl.BlockSpec((d, tf), lambda i, f: (0, f)),
            pl.BlockSpec((d, tf), lambda i, f: (0, f)),
            pl.BlockSpec((tf, d), lambda i, f: (f, 0)),
            pl.BlockSpec((1, d), lambda i, f: (0, 0)),
            pl.BlockSpec((1, d), lambda i, f: (0, 0)),
        ],
        out_specs=out_specs,
        out_shape=out_shape,
        scratch_shapes=[pltpu.VMEM((tm, d), BF16), pltpu.VMEM((tm, d), F32)],
        compiler_params=_cparams(("parallel", "arbitrary")),
        name="ffn_ln",
    )(x, w1, w3, w2, g, b)


def _glu_kernel(x_ref, w1_ref, w3_ref, o_ref, xb_ref):
    @pl.when(pl.program_id(1) == 0)
    def _():
        xb_ref[...] = x_ref[...].astype(BF16)

    xb = xb_ref[...]
    o_ref[...] = (_silu(_dot(xb, w1_ref[...])) * _dot(xb, w3_ref[...])).astype(BF16)


def _glu(x, w1, w3, *, tm=1024, tf=512):
    m, d = x.shape
    dff = w1.shape[1]
    tm = min(tm, m)
    return pl.pallas_call(
        _glu_kernel,
        grid=(m // tm, dff // tf),
        in_specs=[
            pl.BlockSpec((tm, d), lambda i, f: (i, 0)),
            pl.BlockSpec((d, tf), lambda i, f: (0, f)),
            pl.BlockSpec((d, tf), lambda i, f: (0, f)),
        ],
        out_specs=pl.BlockSpec((tm, tf), lambda i, f: (i, f)),
        out_shape=jax.ShapeDtypeStruct((m, dff), BF16),
        scratch_shapes=[pltpu.VMEM((tm, d), BF16)],
        compiler_params=_cparams(("parallel", "arbitrary")),
        name="glu",
    )(x, w1, w3)


def _in_proj_kernel(x_ref, w_ref, gain_ref, rc_ref, ra_ref, rb_ref, o_ref):
    n = pl.program_id(1)
    acc = _dot(x_ref[...], w_ref[...])

    @pl.when(jnp.logical_and(n >= QKV_BLK0, n < V_BLK))
    def _():
        rc = rc_ref[...]
        ra = ra_ref[...]
        rb = rb_ref[...]
        for hh in range(COL_BLK // HEAD_DIM):
            sl = slice(hh * HEAD_DIM, (hh + 1) * HEAD_DIM)
            xh = acc[:, sl]
            ms = jnp.mean(xh * xh, axis=-1, keepdims=True)
            xh = xh * lax.rsqrt(ms + RMS_EPS) * gain_ref[:, sl]
            r = xh * rc + pltpu.roll(xh, 96, 1) * ra + pltpu.roll(xh, 32, 1) * rb
            o_ref[:, sl] = r.astype(BF16)

    @pl.when(jnp.logical_or(n < QKV_BLK0, n >= V_BLK))
    def _():
        o_ref[...] = acc.astype(BF16)


def _in_proj(xb, w, gains, rc, ra, rb, *, seq, tm=1024):
    m, d = xb.shape
    tm = min(tm, seq)
    t_per_seq = seq // tm
    n_gain_blk = gains.shape[1] // COL_BLK
    return pl.pallas_call(
        _in_proj_kernel,
        grid=(m // tm, H_BLKS),
        in_specs=[
            pl.BlockSpec((tm, d), lambda i, n: (i, 0)),
            pl.BlockSpec((d, COL_BLK), lambda i, n: (0, n)),
            pl.BlockSpec((1, COL_BLK), lambda i, n: (0, jnp.clip(n - QKV_BLK0, 0, n_gain_blk - 1))),
            pl.BlockSpec((tm, HEAD_DIM), lambda i, n: (i % t_per_seq, 0)),
            pl.BlockSpec((tm, HEAD_DIM), lambda i, n: (i % t_per_seq, 0)),
            pl.BlockSpec((tm, HEAD_DIM), lambda i, n: (i % t_per_seq, 0)),
        ],
        out_specs=pl.BlockSpec((tm, COL_BLK), lambda i, n: (i, n)),
        out_shape=jax.ShapeDtypeStruct((m, H_W), BF16),
        compiler_params=_cparams(("parallel", "arbitrary")),
        name="in_proj",
    )(xb, w, gains, rc, ra, rb)


def _matmul_kernel(x_ref, w_ref, o_ref):
    o_ref[...] = _dot(x_ref[...], w_ref[...]).astype(o_ref.dtype)


def _matmul(xb, w, out_dtype, *, tm, tn):
    m, d = xb.shape
    n = w.shape[1]
    tm = min(tm, m)
    return pl.pallas_call(
        _matmul_kernel,
        grid=(m // tm, n // tn),
        in_specs=[pl.BlockSpec((tm, d), lambda i, j: (i, 0)),
                  pl.BlockSpec((d, tn), lambda i, j: (0, j))],
        out_specs=pl.BlockSpec((tm, tn), lambda i, j: (i, j)),
        out_shape=jax.ShapeDtypeStruct((m, n), out_dtype),
        compiler_params=_cparams(("parallel", "arbitrary")),
        name="matmul",
    )(xb, w)


def _attn_kernel(q_ref, k_ref, v_ref, o_ref, v1_ref, *, tq, tk):
    seq = k_ref.shape[0]

    @pl.when(pl.program_id(2) == 0)
    def _():
        v1_ref[:, :HEAD_DIM] = v_ref[...]
        v1_ref[:, HEAD_DIM:] = jnp.ones((seq, HEAD_DIM), BF16)

    q = jnp.concatenate([q_ref[:, hh * HEAD_DIM:(hh + 1) * HEAD_DIM] for hh in range(GQA_REP)], axis=0)
    rows = GQA_REP * tq
    m_i = jnp.full((rows, 1), -jnp.inf, F32)
    acc = jnp.zeros((rows, 2 * HEAD_DIM), F32)
    for j in range(seq // tk):
        s = _dot_nt(q, k_ref[j * tk:(j + 1) * tk, :])
        m_new = jnp.maximum(m_i, jnp.max(s, axis=-1, keepdims=True))
        p = jnp.exp2(s - m_new).astype(BF16)
        acc = jnp.exp2(m_i - m_new) * acc + _dot(p, v1_ref[j * tk:(j + 1) * tk, :])
        m_i = m_new
    o = (acc[:, :HEAD_DIM] / acc[:, HEAD_DIM:HEAD_DIM + 1]).astype(BF16)
    for hh in range(GQA_REP):
        o_ref[:, hh * HEAD_DIM:(hh + 1) * HEAD_DIM] = o[hh * tq:(hh + 1) * tq]


def _attention(h3, *, tq=256, tk=512):
    bsz, seq, _ = h3.shape
    tk = min(tk, seq)
    k_blk0 = (QKV_BLK0 * COL_BLK + Q_W) // HEAD_DIM
    v_blk0 = V_BLK * COL_BLK // HEAD_DIM
    return pl.pallas_call(
        functools.partial(_attn_kernel, tq=tq, tk=tk),
        grid=(bsz, N_KV_HEADS, seq // tq),
        in_specs=[
            pl.BlockSpec((None, tq, GQA_REP * HEAD_DIM), lambda b, kv, i: (b, i, QKV_BLK0 + kv)),
            pl.BlockSpec((None, seq, HEAD_DIM), lambda b, kv, i: (b, 0, k_blk0 + kv)),
            pl.BlockSpec((None, seq, HEAD_DIM), lambda b, kv, i: (b, 0, v_blk0 + kv)),
        ],
        out_specs=pl.BlockSpec((None, tq, GQA_REP * HEAD_DIM), lambda b, kv, i: (b, i, kv)),
        out_shape=jax.ShapeDtypeStruct((bsz, seq, Q_W), BF16),
        scratch_shapes=[pltpu.VMEM((seq, 2 * HEAD_DIM), BF16)],
        compiler_params=_cparams(("parallel", "parallel", "arbitrary")),
        name="attention",
    )(h3, h3, h3)


CONV_HALO = 8
CONV_SUB = 128
CONV_PAD = CONV_W // 2


def _conv_kernel(prev_ref, main_ref, next_ref, w_ref, b_ref, o_ref, win_ref, *, rt):
    r = pl.program_id(1)
    first = r == 0
    last = r == pl.num_programs(1) - 1
    win_ref[0:CONV_HALO, :] = jnp.where(first, 0.0, prev_ref[...].astype(F32))
    win_ref[CONV_HALO:CONV_HALO + rt, :] = main_ref[...].astype(F32)
    win_ref[CONV_HALO + rt:, :] = jnp.where(last, 0.0, next_ref[...].astype(F32))
    bias = b_ref[...]
    span = CONV_SUB + 2 * CONV_HALO
    for t in range(rt // CONV_SUB):
        win = win_ref[t * CONV_SUB:t * CONV_SUB + span, :]
        acc = bias + w_ref[CONV_PAD:CONV_PAD + 1, :] * win[CONV_HALO:CONV_HALO + CONV_SUB]
        for k in range(CONV_W):
            if k == CONV_PAD:
                continue
            rolled = pltpu.roll(win, (CONV_PAD - k) % span, 0)
            acc = acc + w_ref[k:k + 1, :] * rolled[CONV_HALO:CONV_HALO + CONV_SUB]
        o_ref[t * CONV_SUB:(t + 1) * CONV_SUB, :] = _silu(acc).astype(BF16)


def _conv_silu(h3, conv_w, conv_b, *, rt=512):
    bsz, seq, _ = h3.shape
    rt = min(rt, seq)
    nh = rt // CONV_HALO
    n_halo_blk = seq // CONV_HALO
    return pl.pallas_call(
        functools.partial(_conv_kernel, rt=rt),
        grid=(bsz, seq // rt, XBC_W // COL_BLK),
        in_specs=[
            pl.BlockSpec((None, CONV_HALO, COL_BLK),
                         lambda b, r, c: (b, jnp.maximum(r * nh - 1, 0), XBC_BLK0 + c)),
            pl.BlockSpec((None, rt, COL_BLK), lambda b, r, c: (b, r, XBC_BLK0 + c)),
            pl.BlockSpec((None, CONV_HALO, COL_BLK),
                         lambda b, r, c: (b, jnp.minimum((r + 1) * nh, n_halo_blk - 1), XBC_BLK0 + c)),
            pl.BlockSpec((CONV_W, COL_BLK), lambda b, r, c: (0, c)),
            pl.BlockSpec((1, COL_BLK), lambda b, r, c: (0, c)),
        ],
        out_specs=pl.BlockSpec((None, rt, COL_BLK), lambda b, r, c: (b, r, c)),
        out_shape=jax.ShapeDtypeStruct((bsz, seq, XBC_W), BF16),
        scratch_shapes=[pltpu.VMEM((rt + 2 * CONV_HALO, COL_BLK), F32)],
        compiler_params=_cparams(("parallel", "parallel", "arbitrary")),
        name="conv_silu",
    )(h3, h3, h3, conv_w, conv_b)


def _softplus(x):
    return jnp.maximum(x, 0.0) + jnp.log1p(jnp.exp(-jnp.abs(x)))


def _cumsum_pos(t01, a):
    hi, mid, lo = _split3(a)
    t2 = jnp.concatenate([t01, t01], axis=1)
    return _dot(t2, jnp.concatenate([hi, mid], axis=0)) + _dot(t01, lo)


def _ssd_kernel(*refs, backward):
    if backward:
        (xs_ref, bm_ref, cm_ref, dt_ref, bias_ref, alog_ref, ex_ref, yf_ref, z_ref, dvec_ref, ng_ref,
         o_ref, h_ref) = refs
    else:
        xs_ref, bm_ref, cm_ref, dt_ref, bias_ref, alog_ref, ex_ref, o_ref, h_ref = refs

    @pl.when(pl.program_id(1) == 0)
    def _():
        h_ref[...] = jnp.zeros_like(h_ref)

    lane = lax.broadcasted_iota(jnp.int32, (CHUNK, LANE), 1)
    row = lax.broadcasted_iota(jnp.int32, (CHUNK, LANE), 0)
    nh = N_SSD_HEADS
    hpg = HEADS_PER_GROUP

    dt = _softplus(dt_ref[...] + bias_ref[...])
    a = dt * (-jnp.exp(alog_ref[...]))
    tri_incl = lane <= row
    t_lo = jnp.where(tri_incl, 1.0, 0.0).astype(BF16)
    t_up = jnp.where(lane >= row, 1.0, 0.0).astype(BF16)
    cs = jnp.where(lane < nh, _cumsum_pos(t_lo, a), _cumsum_pos(t_up, a))

    off = nh if backward else 0
    if backward:
        edge = cs[0:1, :]
    else:
        edge = cs[CHUNK - 1:CHUNK, :]
    used = jnp.logical_and(lane >= off, lane < off + nh)
    e_pos = jnp.where(used, jnp.exp(jnp.where(used, cs, 0.0)), 0.0)
    w_pos = jnp.where(used, jnp.exp(jnp.where(used, edge - cs, 0.0)) * dt, 0.0)
    st = jnp.concatenate([e_pos, w_pos], axis=0)
    st_hi = st.astype(BF16)
    st_mid = (st - st_hi.astype(F32)).astype(BF16)
    st2 = jnp.concatenate([st_hi, st_mid], axis=1)
    if not backward:
        cs_t = cs.T
        dt_t = dt.T

    for g in range(N_SSD_GROUPS):
        sl = slice(g * GROUP_W, (g + 1) * GROUP_W)
        sb = slice(g * D_STATE, (g + 1) * D_STATE)
        xs_f = xs_ref[:, sl].astype(F32)
        bm = bm_ref[:, sb]
        cm = cm_ref[:, sb]
        ex = _dot(st2, ex_ref[g])
        e_exp = ex[:CHUNK]
        w_exp = ex[CHUNK:]
        chunk_decay = e_exp[0:1, :] if backward else e_exp[CHUNK - 1:CHUNK, :]
        h = h_ref[g]
        y_off = _dot(cm, h.astype(BF16)) * e_exp
        s_t = _dot_tn(bm, (xs_f * w_exp).astype(BF16))
        h_ref[g] = h * chunk_decay + s_t

        if not backward:
            cb = _dot_nt(cm, bm)
            pairs = []
            for pp in range(hpg // 2):
                x_pair = xs_f[:, pp * LANE:(pp + 1) * LANE]
                ms = []
                for hh in (g * hpg + 2 * pp, g * hpg + 2 * pp + 1):
                    arg = jnp.where(tri_incl,
                                    cs[:, hh:hh + 1] - cs_t[hh:hh + 1, :],
                                    cs[:, nh + hh:nh + hh + 1] - cs_t[nh + hh:nh + hh + 1, :])
                    dts = jnp.where(tri_incl, dt_t[hh:hh + 1, :], dt_t[nh + hh:nh + hh + 1, :])
                    ms.append((cb * jnp.exp(arg) * dts).astype(BF16))
                x_a = jnp.where(lane < SSD_HEAD_DIM, x_pair, 0.0).astype(BF16)
                x_b = jnp.where(lane >= SSD_HEAD_DIM, x_pair, 0.0).astype(BF16)
                pairs.append(_dot(jnp.concatenate(ms, axis=1), jnp.concatenate([x_a, x_b], axis=0)))
            o_ref[:, sl] = jnp.concatenate(pairs, axis=1) + y_off
        else:
            y = yf_ref[:, sl] + y_off + dvec_ref[:, sl] * xs_f
            y = y * _silu(z_ref[:, sl].astype(F32))
            y = y * lax.rsqrt(jnp.mean(y * y, axis=-1, keepdims=True) + RMS_EPS) * ng_ref[:, sl]
            o_ref[:, sl] = y.astype(BF16)


def _expand_mats(off):
    k = (jnp.arange(2 * LANE) % LANE)[None, :, None]
    col = (jnp.arange(GROUP_W) // SSD_HEAD_DIM)[None, None, :]
    g = jnp.arange(N_SSD_GROUPS)[:, None, None]
    return (k == off + HEADS_PER_GROUP * g + col).astype(BF16)


def _ssd(xbc3, dt3, bias, alog, h3, dvec, ng):
    bsz, seq, _ = xbc3.shape
    nc = seq // CHUNK
    g_n = N_SSD_GROUPS
    scratch = [pltpu.VMEM((g_n, D_STATE, GROUP_W), F32)]
    sem = ("parallel", "arbitrary")

    def common_specs(cidx):
        return [
            pl.BlockSpec((None, CHUNK, D_INNER), lambda b, j: (b, cidx(j), 0)),
            pl.BlockSpec((None, CHUNK, BC_W), lambda b, j: (b, cidx(j), D_INNER // BC_W)),
            pl.BlockSpec((None, CHUNK, BC_W), lambda b, j: (b, cidx(j), D_INNER // BC_W + 1)),
            pl.BlockSpec((None, CHUNK, LANE), lambda b, j: (b, cidx(j), 0)),
            pl.BlockSpec((1, LANE), lambda b, j: (0, 0)),
            pl.BlockSpec((1, LANE), lambda b, j: (0, 0)),
            pl.BlockSpec((g_n, 2 * LANE, GROUP_W), lambda b, j: (0, 0, 0)),
        ]

    fwd = lambda j: j
    y_f = pl.pallas_call(
        functools.partial(_ssd_kernel, backward=False),
        grid=(bsz, nc),
        in_specs=common_specs(fwd),
        out_specs=pl.BlockSpec((None, CHUNK, D_INNER), lambda b, j: (b, j, 0)),
        out_shape=jax.ShapeDtypeStruct((bsz, seq, D_INNER), F32),
        scratch_shapes=scratch,
        compiler_params=_cparams(sem),
        name="ssd_fwd",
    )(xbc3, xbc3, xbc3, dt3, bias, alog, _expand_mats(0))

    bwd = lambda j: nc - 1 - j
    return pl.pallas_call(
        functools.partial(_ssd_kernel, backward=True),
        grid=(bsz, nc),
        in_specs=common_specs(bwd) + [
            pl.BlockSpec((None, CHUNK, D_INNER), lambda b, j: (b, bwd(j), 0)),
            pl.BlockSpec((None, CHUNK, D_INNER), lambda b, j: (b, bwd(j), Z_BLK0 * COL_BLK // D_INNER)),
            pl.BlockSpec((1, D_INNER), lambda b, j: (0, 0)),
            pl.BlockSpec((1, D_INNER), lambda b, j: (0, 0)),
        ],
        out_specs=pl.BlockSpec((None, CHUNK, D_INNER), lambda b, j: (b, bwd(j), 0)),
        out_shape=jax.ShapeDtypeStruct((bsz, seq, D_INNER), BF16),
        scratch_shapes=scratch,
        compiler_params=_cparams(sem),
        name="ssd_bwd",
    )(xbc3, xbc3, xbc3, dt3, bias, alog, _expand_mats(N_SSD_HEADS), y_f, h3, dvec, ng)


def _branch_mix_kernel(ya_ref, ys_ref, ga_ref, gs_ref, wa_ref, ws_ref, o_ref):
    pa = jax.nn.sigmoid(ga_ref[...].astype(F32)) * _dot(ya_ref[...], wa_ref[...])
    ps = jax.nn.sigmoid(gs_ref[...].astype(F32)) * _dot(ys_ref[...], ws_ref[...])
    o_ref[...] = (pa + ps).astype(BF16)


def _branch_mix(y_attn, y_ssd, h, wa, ws, *, tm=1024, tn=512):
    m = y_attn.shape[0]
    d = wa.shape[1]
    tm = min(tm, m)
    n_blk = d // tn
    gate0 = GATE_BLK0 * COL_BLK // tn
    return pl.pallas_call(
        _branch_mix_kernel,
        grid=(m // tm, n_blk),
        in_specs=[
            pl.BlockSpec((tm, Q_W), lambda i, n: (i, 0)),
            pl.BlockSpec((tm, D_INNER), lambda i, n: (i, 0)),
            pl.BlockSpec((tm, tn), lambda i, n: (i, gate0 + n)),
            pl.BlockSpec((tm, tn), lambda i, n: (i, gate0 + n_blk + n)),
            pl.BlockSpec((Q_W, tn), lambda i, n: (0, n)),
            pl.BlockSpec((D_INNER, tn), lambda i, n: (0, n)),
        ],
        out_specs=pl.BlockSpec((tm, tn), lambda i, n: (i, n)),
        out_shape=jax.ShapeDtypeStruct((m, d), BF16),
        compiler_params=_cparams(("parallel", "arbitrary")),
        name="branch_mix",
    )(y_attn, y_ssd, h, h, wa, ws)


def _proj_ln_kernel(a_ref, x_ref, w_ref, g_ref, b_ref, o_ref, ob_ref, *, scale):
    y = _dot(a_ref[...], w_ref[...])
    if scale != 1.0:
        y = scale * y
    o = _layer_norm(ALPHA * x_ref[...] + y, g_ref[...], b_ref[...])
    o_ref[...] = o
    ob_ref[...] = o.astype(BF16)


def _proj_ln(a, x, w, g, b, *, scale=1.0, tm=512):
    m, d = x.shape
    k = a.shape[1]
    tm = min(tm, m)
    return pl.pallas_call(
        functools.partial(_proj_ln_kernel, scale=scale),
        grid=(m // tm,),
        in_specs=[
            pl.BlockSpec((tm, k), lambda i: (i, 0)),
            pl.BlockSpec((tm, d), lambda i: (i, 0)),
            pl.BlockSpec((k, d), lambda i: (0, 0), pipeline_mode=pl.Buffered(1)),
            pl.BlockSpec((1, d), lambda i: (0, 0)),
            pl.BlockSpec((1, d), lambda i: (0, 0)),
        ],
        out_specs=[pl.BlockSpec((tm, d), lambda i: (i, 0)),
                   pl.BlockSpec((tm, d), lambda i: (i, 0))],
        out_shape=[jax.ShapeDtypeStruct((m, d), F32), jax.ShapeDtypeStruct((m, d), BF16)],
        compiler_params=_cparams(("parallel",)),
        name="proj_ln",
    )(a, x, w, g, b)


def _cross_kernel(x_ref, xb_ref, wq_ref, kv_ref, wo_ref, g_ref, b_ref, o_ref):
    scale = MEM_HEAD_DIM ** -0.5
    q = (_dot(xb_ref[...], wq_ref[...]) * scale).astype(BF16)
    outs = []
    for hh in range(N_MEM_HEADS):
        sl = slice(hh * MEM_HEAD_DIM, (hh + 1) * MEM_HEAD_DIM)
        k = kv_ref[:, sl]
        v = kv_ref[:, MEM_W + hh * MEM_HEAD_DIM:MEM_W + (hh + 1) * MEM_HEAD_DIM]
        s = _dot_nt(q[:, sl], k)
        p = jnp.exp(s - jnp.max(s, axis=-1, keepdims=True))
        p = p / jnp.sum(p, axis=-1, keepdims=True)
        outs.append(_dot(p.astype(BF16), v))
    o_att = jnp.concatenate(outs, axis=1).astype(BF16)
    y = ALPHA * x_ref[...] + _dot(o_att, wo_ref[...])
    o_ref[...] = _layer_norm(y, g_ref[...], b_ref[...])


def _cross(x2, x2b, wq, kv3, wo, g, b, *, seq, tm=512):
    m, d = x2.shape
    tm = min(tm, seq)
    t_per_seq = seq // tm
    mem_len = kv3.shape[1]
    return pl.pallas_call(
        _cross_kernel,
        grid=(m // tm,),
        in_specs=[
            pl.BlockSpec((tm, d), lambda i: (i, 0)),
            pl.BlockSpec((tm, d), lambda i: (i, 0)),
            pl.BlockSpec((d, MEM_W), lambda i: (0, 0)),
            pl.BlockSpec((None, mem_len, 2 * MEM_W), lambda i: (i // t_per_seq, 0, 0)),
            pl.BlockSpec((MEM_W, d), lambda i: (0, 0)),
            pl.BlockSpec((1, d), lambda i: (0, 0)),
            pl.BlockSpec((1, d), lambda i: (0, 0)),
        ],
        out_specs=pl.BlockSpec((tm, d), lambda i: (i, 0)),
        out_shape=jax.ShapeDtypeStruct((m, d), F32),
        compiler_params=_cparams(("parallel",)),
        name="cross_attn",
    )(x2, x2b, wq, kv3, wo, g, b)


def _rope_tables(seq):
    t = jnp.arange(seq)
    row = (t // GRID_W).astype(F32)
    col = (t % GRID_W).astype(F32)
    inv_freq = ROPE_THETA ** (-jnp.arange(ROPE_AXIS_DIM // 2, dtype=F32) * 2.0 / ROPE_AXIS_DIM)
    ang_r = row[:, None] * inv_freq[None, :]
    ang_c = col[:, None] * inv_freq[None, :]
    cr, sr, cc, sc = jnp.cos(ang_r), jnp.sin(ang_r), jnp.cos(ang_c), jnp.sin(ang_c)
    zr = jnp.zeros_like(sr)
    rc = jnp.concatenate([cr, cr, cc, cc], axis=1)
    ra = jnp.concatenate([-sr, zr, -sc, zr], axis=1)
    rb = jnp.concatenate([zr, sr, zr, sc], axis=1)
    return rc, ra, rb


def kernel(x, mem, ffn1_w1, ffn1_w3, ffn1_w2, ln1_g, ln1_b, w_in, q_norm_g, k_norm_g, conv_w, conv_b,
           dt_bias, a_log, ssd_d, ssd_norm_g, w_branch_attn, w_branch_ssd, w_mix_out, ln2_g, ln2_b,
           mem_wq, mem_wk, mem_wv, mem_wo, ln3_g, ln3_b, ffn2_w1, ffn2_w3, ffn2_w2, ln4_g, ln4_b):
    bsz, seq, d = x.shape
    m = bsz * seq
    mem_len = mem.shape[1]
    bf = lambda t: t.astype(BF16)
    rc, ra, rb = _rope_tables(seq)
    xc = x.reshape(m, d)
    for l in range(DEPTH):
        row = lambda v: v[l].reshape(1, -1)
        act = _glu(xc, bf(ffn1_w1[l]), bf(ffn1_w3[l]))
        x1, x1b = _proj_ln(act, xc, bf(ffn1_w2[l]), row(ln1_g), row(ln1_b), scale=0.5, tm=256)
        z0 = Q_W + 2 * KV_W
        xbc0 = z0 + D_INNER
        dt0 = xbc0 + XBC_W
        w_main = bf(jnp.concatenate([w_in[l][:, z0:dt0], w_in[l][:, :z0], w_in[l][:, dt0 + DT_W:]],
                                    axis=1))
        gains = jnp.concatenate([jnp.tile(q_norm_g[l] * (HEAD_DIM ** -0.5 * LOG2E), N_Q_HEADS),
                                 jnp.tile(k_norm_g[l], N_KV_HEADS)]).reshape(1, -1)
        h = _in_proj(x1b, w_main, gains, rc, ra, rb, seq=seq)
        dt_raw = _matmul(x1b, bf(w_in[l][:, dt0:dt0 + DT_W]), F32, tm=1024, tn=DT_W)
        h3 = h.reshape(bsz, seq, H_W)
        y_attn = _attention(h3)
        xbc = _conv_silu(h3, conv_w[l], conv_b[l].reshape(1, -1))
        y_ssd = _ssd(xbc, dt_raw.reshape(bsz, seq, DT_W), dt_bias[l].reshape(1, DT_W),
                     a_log[l].reshape(1, DT_W), h3,
                     jnp.repeat(ssd_d[l], SSD_HEAD_DIM).reshape(1, D_INNER),
                     ssd_norm_g[l].reshape(1, D_INNER))
        merged = _branch_mix(y_attn.reshape(m, Q_W), y_ssd.reshape(m, D_INNER), h,
                             bf(w_branch_attn[l]), bf(w_branch_ssd[l]))
        x2, x2b = _proj_ln(merged, x1, bf(w_mix_out[l]), row(ln2_g), row(ln2_b))
        w_kv = bf(jnp.concatenate([mem_wk[l], mem_wv[l]], axis=1))
        kv = _matmul(bf(mem.reshape(bsz * mem_len, d)), w_kv, BF16, tm=bsz * mem_len, tn=COL_BLK)
        x3 = _cross(x2, x2b, bf(mem_wq[l]), kv.reshape(bsz, mem_len, 2 * MEM_W), bf(mem_wo[l]),
                       row(ln3_g), row(ln3_b), seq=seq)
        (xc,) = _ffn_ln(x3, bf(ffn2_w1[l]), bf(ffn2_w3[l]), bf(ffn2_w2[l]), row(ln4_g), row(ln4_b),
                        emit_bf16=False)
    return xc.reshape(bsz, seq, d)
```

```python
import functools
import math

import jax
import jax.numpy as jnp
import numpy as np
from jax import lax
from jax.experimental import pallas as pl
from jax.experimental.pallas import tpu as pltpu

F32 = jnp.float32
BF16 = jnp.bfloat16

D_MODEL = 2048
DEPTH = 1
GRID_W = 64
HEAD_DIM = 128
N_Q_HEADS = 16
N_KV_HEADS = 4
GQA_REP = 4
ROPE_AXIS_DIM = 64
ROPE_THETA = 10000.0
D_INNER = 4096
SSD_HEAD_DIM = 64
N_SSD_HEADS = 64
N_SSD_GROUPS = 8
HEADS_PER_GROUP = N_SSD_HEADS // N_SSD_GROUPS
GROUP_W = D_INNER // N_SSD_GROUPS
D_STATE = 128
CONV_W = 5
CHUNK = 128
N_MEM_HEADS = 4
MEM_HEAD_DIM = 128
D_FF = 5632
LN_EPS = 1e-5
RMS_EPS = 1e-6
Q_W = 2048
KV_W = 512
BC_W = 1024
XBC_W = 6144
DT_W = 128
MEM_W = 512
ALPHA = (2.0 * DEPTH) ** 0.25
LOG2E = math.log2(math.e)

LANE = 128
COL_BLK = 512
Z_BLK0 = 0
XBC_BLK0 = Z_BLK0 + D_INNER // COL_BLK
QKV_BLK0 = XBC_BLK0 + XBC_W // COL_BLK
V_BLK = QKV_BLK0 + (Q_W + KV_W) // COL_BLK
H_BLKS = V_BLK + 1
H_W = H_BLKS * COL_BLK
SRC_QKV_BLKS = (Q_W + 2 * KV_W) // COL_BLK
SRC_ROPE_BLKS = (Q_W + KV_W) // COL_BLK

VMEM_LIMIT = 56 * 1024 * 1024


def _cparams(sem):
    return pltpu.CompilerParams(dimension_semantics=sem, vmem_limit_bytes=VMEM_LIMIT)


def _dot(a, b):
    return jnp.dot(a, b, preferred_element_type=F32)


def _dot_nt(a, b):
    return lax.dot_general(a, b, (((1,), (1,)), ((), ())), preferred_element_type=F32)


def _dot_tn(a, b):
    return lax.dot_general(a, b, (((0,), (0,)), ((), ())), preferred_element_type=F32)


def _split3(v):
    hi = v.astype(BF16)
    r1 = v - hi.astype(F32)
    mid = r1.astype(BF16)
    lo = (r1 - mid.astype(F32)).astype(BF16)
    return hi, mid, lo


def _layer_norm(y, g, b):
    mu = jnp.mean(y, axis=-1, keepdims=True)
    d = y - mu
    var = jnp.mean(d * d, axis=-1, keepdims=True)
    return d * lax.rsqrt(var + LN_EPS) * g + b


def _silu(x):
    return x * jax.nn.sigmoid(x)


def _glu_kernel(x_ref, w1_ref, w3_ref, o_ref, xb_ref):
    @pl.when(pl.program_id(1) == 0)
    def _():
        xb_ref[...] = x_ref[...].astype(BF16)

    xb = xb_ref[...]
    o_ref[...] = (_silu(_dot(xb, w1_ref[...])) * _dot(xb, w3_ref[...])).astype(BF16)


def _glu(x, w1, w3, *, tm=1024, tf=512):
    m, d = x.shape
    dff = w1.shape[1]
    tm = min(tm, m)
    return pl.pallas_call(
        _glu_kernel,
        grid=(m // tm, dff // tf),
        in_specs=[
            pl.BlockSpec((tm, d), lambda i, f: (i, 0)),
            pl.BlockSpec((d, tf), lambda i, f: (0, f)),
            pl.BlockSpec((d, tf), lambda i, f: (0, f)),
        ],
        out_specs=pl.BlockSpec((tm, tf), lambda i, f: (i, f)),
        out_shape=jax.ShapeDtypeStruct((m, dff), BF16),
        scratch_shapes=[pltpu.VMEM((tm, d), BF16)],
        compiler_params=_cparams(("parallel", "arbitrary")),
        name="glu",
    )(x, w1, w3)


def _in_proj_kernel(x_ref, w_ref, gain_ref, rc_ref, ra_ref, rb_ref, o_ref):
    n = pl.program_id(1)
    acc = _dot(x_ref[...], w_ref[...])

    @pl.when(n < SRC_ROPE_BLKS)
    def _():
        rc = rc_ref[...]
        ra = ra_ref[...]
        rb = rb_ref[...]
        for hh in range(COL_BLK // HEAD_DIM):
            sl = slice(hh * HEAD_DIM, (hh + 1) * HEAD_DIM)
            xh = acc[:, sl]
            ms = jnp.mean(xh * xh, axis=-1, keepdims=True)
            xh = xh * lax.rsqrt(ms + RMS_EPS) * gain_ref[:, sl]
            r = xh * rc + pltpu.roll(xh, 96, 1) * ra + pltpu.roll(xh, 32, 1) * rb
            o_ref[:, sl] = r.astype(BF16)

    @pl.when(n >= SRC_ROPE_BLKS)
    def _():
        o_ref[...] = acc.astype(BF16)


def _in_proj(xb, w, gains, rc, ra, rb, *, seq, tm=2048):
    m, d = xb.shape
    tm = min(tm, seq)
    t_per_seq = seq // tm
    n_gain_blk = gains.shape[1] // COL_BLK
    out_blk = lambda n: jnp.where(n < SRC_QKV_BLKS, n + QKV_BLK0, n - SRC_QKV_BLKS)
    return pl.pallas_call(
        _in_proj_kernel,
        grid=(m // tm, H_BLKS),
        in_specs=[
            pl.BlockSpec((tm, d), lambda i, n: (i, 0)),
            pl.BlockSpec((d, COL_BLK), lambda i, n: (0, n)),
            pl.BlockSpec((1, COL_BLK), lambda i, n: (0, jnp.minimum(n, n_gain_blk - 1))),
            pl.BlockSpec((tm, HEAD_DIM), lambda i, n: (i % t_per_seq, 0)),
            pl.BlockSpec((tm, HEAD_DIM), lambda i, n: (i % t_per_seq, 0)),
            pl.BlockSpec((tm, HEAD_DIM), lambda i, n: (i % t_per_seq, 0)),
        ],
        out_specs=pl.BlockSpec((tm, COL_BLK), lambda i, n: (i, out_blk(n))),
        out_shape=jax.ShapeDtypeStruct((m, H_W), BF16),
        compiler_params=_cparams(("parallel", "arbitrary")),
        name="in_proj",
    )(xb, w, gains, rc, ra, rb)


def _matmul_kernel(x_ref, w_ref, o_ref):
    o_ref[...] = _dot(x_ref[...], w_ref[...]).astype(o_ref.dtype)


def _matmul(xb, w, out_dtype, *, tm, tn, n_out=None, col_blk0=0):
    m, d = xb.shape
    n = w.shape[1] if n_out is None else n_out
    tm = min(tm, m)
    return pl.pallas_call(
        _matmul_kernel,
        grid=(m // tm, n // tn),
        in_specs=[pl.BlockSpec((tm, d), lambda i, j: (i, 0)),
                  pl.BlockSpec((d, tn), lambda i, j: (0, col_blk0 + j))],
        out_specs=pl.BlockSpec((tm, tn), lambda i, j: (i, j)),
        out_shape=jax.ShapeDtypeStruct((m, n), out_dtype),
        compiler_params=_cparams(("parallel", "arbitrary")),
        name="matmul",
    )(xb, w)


def _attn_kernel(q_ref, k_ref, v_ref, o_ref, v1_ref, *, tq, tk):
    seq = k_ref.shape[0]

    @pl.when(pl.program_id(2) == 0)
    def _():
        v1_ref[:, :HEAD_DIM] = v_ref[...]
        v1_ref[:, HEAD_DIM:] = jnp.ones((seq, HEAD_DIM), BF16)

    q = jnp.concatenate([q_ref[:, hh * HEAD_DIM:(hh + 1) * HEAD_DIM] for hh in range(GQA_REP)], axis=0)
    rows = GQA_REP * tq
    m_i = jnp.full((rows, 1), -jnp.inf, F32)
    acc = jnp.zeros((rows, 2 * HEAD_DIM), F32)
    for j in range(seq // tk):
        s = _dot_nt(q, k_ref[j * tk:(j + 1) * tk, :])
        m_new = jnp.maximum(m_i, jnp.max(s, axis=-1, keepdims=True))
        p = jnp.exp2(s - m_new).astype(BF16)
        acc = jnp.exp2(m_i - m_new) * acc + _dot(p, v1_ref[j * tk:(j + 1) * tk, :])
        m_i = m_new
    o = (acc[:, :HEAD_DIM] / acc[:, HEAD_DIM:HEAD_DIM + 1]).astype(BF16)
    for hh in range(GQA_REP):
        o_ref[:, hh * HEAD_DIM:(hh + 1) * HEAD_DIM] = o[hh * tq:(hh + 1) * tq]


def _attention(h3, *, tq=256, tk=512):
    bsz, seq, _ = h3.shape
    tk = min(tk, seq)
    k_blk0 = (QKV_BLK0 * COL_BLK + Q_W) // HEAD_DIM
    v_blk0 = V_BLK * COL_BLK // HEAD_DIM
    return pl.pallas_call(
        functools.partial(_attn_kernel, tq=tq, tk=tk),
        grid=(bsz, N_KV_HEADS, seq // tq),
        in_specs=[
            pl.BlockSpec((None, tq, GQA_REP * HEAD_DIM), lambda b, kv, i: (b, i, QKV_BLK0 + kv)),
            pl.BlockSpec((None, seq, HEAD_DIM), lambda b, kv, i: (b, 0, k_blk0 + kv)),
            pl.BlockSpec((None, seq, HEAD_DIM), lambda b, kv, i: (b, 0, v_blk0 + kv)),
        ],
        out_specs=pl.BlockSpec((None, tq, GQA_REP * HEAD_DIM), lambda b, kv, i: (b, i, kv)),
        out_shape=jax.ShapeDtypeStruct((bsz, seq, Q_W), BF16),
        scratch_shapes=[pltpu.VMEM((seq, 2 * HEAD_DIM), BF16)],
        compiler_params=_cparams(("parallel", "parallel", "arbitrary")),
        name="attention",
    )(h3, h3, h3)


CONV_HALO = 8
CONV_SUB = 128
CONV_PAD = CONV_W // 2


def _conv_kernel(prev_ref, main_ref, next_ref, w_ref, b_ref, o_ref, win_ref, *, rt):
    r = pl.program_id(1)
    first = r == 0
    last = r == pl.num_programs(1) - 1
    win_ref[0:CONV_HALO, :] = jnp.where(first, 0.0, prev_ref[...].astype(F32))
    win_ref[CONV_HALO:CONV_HALO + rt, :] = main_ref[...].astype(F32)
    win_ref[CONV_HALO + rt:, :] = jnp.where(last, 0.0, next_ref[...].astype(F32))
    bias = b_ref[...]
    span = CONV_SUB + 2 * CONV_HALO
    for t in range(rt // CONV_SUB):
        win = win_ref[t * CONV_SUB:t * CONV_SUB + span, :]
        acc = bias + w_ref[CONV_PAD:CONV_PAD + 1, :] * win[CONV_HALO:CONV_HALO + CONV_SUB]
        for k in range(CONV_W):
            if k == CONV_PAD:
                continue
            rolled = pltpu.roll(win, (CONV_PAD - k) % span, 0)
            acc = acc + w_ref[k:k + 1, :] * rolled[CONV_HALO:CONV_HALO + CONV_SUB]
        o_ref[t * CONV_SUB:(t + 1) * CONV_SUB, :] = _silu(acc).astype(BF16)


def _conv_silu(h3, conv_w, conv_b, *, rt=1024):
    bsz, seq, _ = h3.shape
    rt = min(rt, seq)
    nh = rt // CONV_HALO
    n_halo_blk = seq // CONV_HALO
    return pl.pallas_call(
        functools.partial(_conv_kernel, rt=rt),
        grid=(bsz, seq // rt, XBC_W // COL_BLK),
        in_specs=[
            pl.BlockSpec((None, CONV_HALO, COL_BLK),
                         lambda b, r, c: (b, jnp.maximum(r * nh - 1, 0), XBC_BLK0 + c)),
            pl.BlockSpec((None, rt, COL_BLK), lambda b, r, c: (b, r, XBC_BLK0 + c)),
            pl.BlockSpec((None, CONV_HALO, COL_BLK),
                         lambda b, r, c: (b, jnp.minimum((r + 1) * nh, n_halo_blk - 1), XBC_BLK0 + c)),
            pl.BlockSpec((CONV_W, COL_BLK), lambda b, r, c: (0, c)),
            pl.BlockSpec((1, COL_BLK), lambda b, r, c: (0, c)),
        ],
        out_specs=pl.BlockSpec((None, rt, COL_BLK), lambda b, r, c: (b, r, c)),
        out_shape=jax.ShapeDtypeStruct((bsz, seq, XBC_W), BF16),
        scratch_shapes=[pltpu.VMEM((rt + 2 * CONV_HALO, COL_BLK), F32)],
        compiler_params=_cparams(("parallel", "parallel", "arbitrary")),
        name="conv_silu",
    )(h3, h3, h3, conv_w, conv_b)


def _softplus(x):
    return jnp.maximum(x, 0.0) + jnp.log1p(jnp.exp(-jnp.abs(x)))


def _cumsum_pos(t01, a):
    hi, mid, lo = _split3(a)
    t2 = jnp.concatenate([t01, t01], axis=1)
    return _dot(t2, jnp.concatenate([hi, mid], axis=0)) + _dot(t01, lo)


def _ssd_kernel(*refs, backward):
    if backward:
        (xs_ref, bm_ref, cm_ref, dt_ref, bias_ref, alog_ref, ex_ref, yf_ref, z_ref, dvec_ref, ng_ref,
         o_ref, h_ref) = refs
    else:
        xs_ref, bm_ref, cm_ref, dt_ref, bias_ref, alog_ref, ex_ref, o_ref, h_ref = refs

    @pl.when(pl.program_id(1) == 0)
    def _():
        h_ref[...] = jnp.zeros_like(h_ref)

    lane = lax.broadcasted_iota(jnp.int32, (CHUNK, LANE), 1)
    row = lax.broadcasted_iota(jnp.int32, (CHUNK, LANE), 0)
    nh = N_SSD_HEADS
    hpg = HEADS_PER_GROUP

    dt = _softplus(dt_ref[...] + bias_ref[...])
    a = dt * (-LOG2E * jnp.exp(alog_ref[...]))
    tri_incl = lane <= row
    t_lo = jnp.where(tri_incl, 1.0, 0.0).astype(BF16)
    t_up = jnp.where(lane >= row, 1.0, 0.0).astype(BF16)
    cs = jnp.where(lane < nh, _cumsum_pos(t_lo, a), _cumsum_pos(t_up, a))

    off = nh if backward else 0
    if backward:
        edge = cs[0:1, :]
    else:
        edge = cs[CHUNK - 1:CHUNK, :]
    used = jnp.logical_and(lane >= off, lane < off + nh)
    e_pos = jnp.where(used, jnp.exp2(jnp.where(used, cs, 0.0)), 0.0)
    w_pos = jnp.where(used, jnp.exp2(jnp.where(used, edge - cs, 0.0)) * dt, 0.0)
    st = jnp.concatenate([e_pos, w_pos], axis=0)
    st_hi = st.astype(BF16)
    st_mid = (st - st_hi.astype(F32)).astype(BF16)
    st2 = jnp.concatenate([st_hi, st_mid], axis=1)
    if not backward:
        src_t = (cs - LOG2E * jnp.log(dt)).T

    for g in range(N_SSD_GROUPS):
        sl = slice(g * GROUP_W, (g + 1) * GROUP_W)
        sb = slice(g * D_STATE, (g + 1) * D_STATE)
        xs_f = xs_ref[:, sl].astype(F32)
        bm = bm_ref[:, sb]
        cm = cm_ref[:, sb]
        ex = _dot(st2, ex_ref[g])
        e_exp = ex[:CHUNK]
        w_exp = ex[CHUNK:]
        chunk_decay = e_exp[0:1, :] if backward else e_exp[CHUNK - 1:CHUNK, :]
        h = h_ref[g]
        y_off = _dot(cm, h.astype(BF16)) * e_exp
        s_t = _dot_tn(bm, (xs_f * w_exp).astype(BF16))
        h_ref[g] = h * chunk_decay + s_t

        if not backward:
            cb = _dot_nt(cm, bm)
            pairs = []
            for pp in range(hpg // 2):
                x_pair = xs_f[:, pp * LANE:(pp + 1) * LANE]
                ms = []
                for hh in (g * hpg + 2 * pp, g * hpg + 2 * pp + 1):
                    arg = jnp.where(tri_incl,
                                    cs[:, hh:hh + 1] - src_t[hh:hh + 1, :],
                                    cs[:, nh + hh:nh + hh + 1] - src_t[nh + hh:nh + hh + 1, :])
                    ms.append((cb * jnp.exp2(arg)).astype(BF16))
                x_a = jnp.where(lane < SSD_HEAD_DIM, x_pair, 0.0).astype(BF16)
                x_b = jnp.where(lane >= SSD_HEAD_DIM, x_pair, 0.0).astype(BF16)
                pairs.append(_dot(jnp.concatenate(ms, axis=1), jnp.concatenate([x_a, x_b], axis=0)))
            o_ref[:, sl] = jnp.concatenate(pairs, axis=1) + y_off
        else:
            y = yf_ref[:, sl] + y_off + dvec_ref[:, sl] * xs_f
            y = y * _silu(z_ref[:, sl].astype(F32))
            y = y * lax.rsqrt(jnp.mean(y * y, axis=-1, keepdims=True) + RMS_EPS) * ng_ref[:, sl]
            o_ref[:, sl] = y.astype(BF16)


def _expand_mats(off):
    k = (np.arange(2 * LANE) % LANE)[None, :, None]
    col = (np.arange(GROUP_W) // SSD_HEAD_DIM)[None, None, :]
    g = np.arange(N_SSD_GROUPS)[:, None, None]
    return jnp.asarray(k == off + HEADS_PER_GROUP * g + col, dtype=BF16)


def _ssd(xbc3, dt3, bias, alog, h3, dvec, ng):
    bsz, seq, _ = xbc3.shape
    nc = seq // CHUNK
    g_n = N_SSD_GROUPS
    scratch = [pltpu.VMEM((g_n, D_STATE, GROUP_W), F32)]
    sem = ("parallel", "arbitrary")

    def common_specs(cidx):
        return [
            pl.BlockSpec((None, CHUNK, D_INNER), lambda b, j: (b, cidx(j), 0)),
            pl.BlockSpec((None, CHUNK, BC_W), lambda b, j: (b, cidx(j), D_INNER // BC_W)),
            pl.BlockSpec((None, CHUNK, BC_W), lambda b, j: (b, cidx(j), D_INNER // BC_W + 1)),
            pl.BlockSpec((None, CHUNK, LANE), lambda b, j: (b, cidx(j), 0)),
            pl.BlockSpec((1, LANE), lambda b, j: (0, 0)),
            pl.BlockSpec((1, LANE), lambda b, j: (0, 0)),
            pl.BlockSpec((g_n, 2 * LANE, GROUP_W), lambda b, j: (0, 0, 0)),
        ]

    fwd = lambda j: j
    y_f = pl.pallas_call(
        functools.partial(_ssd_kernel, backward=False),
        grid=(bsz, nc),
        in_specs=common_specs(fwd),
        out_specs=pl.BlockSpec((None, CHUNK, D_INNER), lambda b, j: (b, j, 0)),
        out_shape=jax.ShapeDtypeStruct((bsz, seq, D_INNER), F32),
        scratch_shapes=scratch,
        compiler_params=_cparams(sem),
        name="ssd_fwd",
    )(xbc3, xbc3, xbc3, dt3, bias, alog, _expand_mats(0))

    bwd = lambda j: nc - 1 - j
    return pl.pallas_call(
        functools.partial(_ssd_kernel, backward=True),
        grid=(bsz, nc),
        in_specs=common_specs(bwd) + [
            pl.BlockSpec((None, CHUNK, D_INNER), lambda b, j: (b, bwd(j), 0)),
            pl.BlockSpec((None, CHUNK, D_INNER), lambda b, j: (b, bwd(j), Z_BLK0 * COL_BLK // D_INNER)),
            pl.BlockSpec((1, D_INNER), lambda b, j: (0, 0)),
            pl.BlockSpec((1, D_INNER), lambda b, j: (0, 0)),
        ],
        out_specs=pl.BlockSpec((None, CHUNK, D_INNER), lambda b, j: (b, bwd(j), 0)),
        out_shape=jax.ShapeDtypeStruct((bsz, seq, D_INNER), BF16),
        scratch_shapes=scratch,
        compiler_params=_cparams(sem),
        name="ssd_bwd",
    )(xbc3, xbc3, xbc3, dt3, bias, alog, _expand_mats(N_SSD_HEADS), y_f, h3, dvec, ng)


def _branch_mix_kernel(ya_ref, ys_ref, ga_ref, gs_ref, wa_ref, ws_ref, o_ref):
    pa = jax.nn.sigmoid(ga_ref[...].astype(F32)) * _dot(ya_ref[...], wa_ref[...])
    ps = jax.nn.sigmoid(gs_ref[...].astype(F32)) * _dot(ys_ref[...], ws_ref[...])
    o_ref[...] = (pa + ps).astype(BF16)


def _branch_mix(y_attn, y_ssd, gate, wa, ws, *, tm=1024, tn=512):
    m = y_attn.shape[0]
    d = wa.shape[1]
    tm = min(tm, m)
    n_blk = d // tn
    return pl.pallas_call(
        _branch_mix_kernel,
        grid=(m // tm, n_blk),
        in_specs=[
            pl.BlockSpec((tm, Q_W), lambda i, n: (i, 0)),
            pl.BlockSpec((tm, D_INNER), lambda i, n: (i, 0)),
            pl.BlockSpec((tm, tn), lambda i, n: (i, n)),
            pl.BlockSpec((tm, tn), lambda i, n: (i, n_blk + n)),
            pl.BlockSpec((Q_W, tn), lambda i, n: (0, n)),
            pl.BlockSpec((D_INNER, tn), lambda i, n: (0, n)),
        ],
        out_specs=pl.BlockSpec((tm, tn), lambda i, n: (i, n)),
        out_shape=jax.ShapeDtypeStruct((m, d), BF16),
        compiler_params=_cparams(("parallel", "arbitrary")),
        name="branch_mix",
    )(y_attn, y_ssd, gate, gate, wa, ws)


def _proj_ln_kernel(a_ref, x_ref, w_ref, g_ref, b_ref, o_ref, *maybe_ob_ref, scale):
    y = _dot(a_ref[...], w_ref[...])
    if scale != 1.0:
        y = scale * y
    o = _layer_norm(ALPHA * x_ref[...] + y, g_ref[...], b_ref[...])
    o_ref[...] = o
    for ob_ref in maybe_ob_ref:
        ob_ref[...] = o.astype(BF16)


def _proj_ln(a, x, w, g, b, *, scale=1.0, tm=512, emit_bf16=True):
    m, d = x.shape
    k = a.shape[1]
    tm = min(tm, m)
    n_out = 2 if emit_bf16 else 1
    return pl.pallas_call(
        functools.partial(_proj_ln_kernel, scale=scale),
        grid=(m // tm,),
        in_specs=[
            pl.BlockSpec((tm, k), lambda i: (i, 0)),
            pl.BlockSpec((tm, d), lambda i: (i, 0)),
            pl.BlockSpec((k, d), lambda i: (0, 0), pipeline_mode=pl.Buffered(1)),
            pl.BlockSpec((1, d), lambda i: (0, 0)),
            pl.BlockSpec((1, d), lambda i: (0, 0)),
        ],
        out_specs=[pl.BlockSpec((tm, d), lambda i: (i, 0))] * n_out,
        out_shape=[jax.ShapeDtypeStruct((m, d), F32), jax.ShapeDtypeStruct((m, d), BF16)][:n_out],
        compiler_params=_cparams(("parallel",)),
        name="proj_ln",
    )(a, x, w, g, b)


def _cross_kernel(x_ref, xb_ref, wq_ref, kv_ref, wo_ref, g_ref, b_ref, o_ref):
    scale = MEM_HEAD_DIM ** -0.5
    q = (_dot(xb_ref[...], wq_ref[...]) * scale).astype(BF16)
    outs = []
    for hh in range(N_MEM_HEADS):
        sl = slice(hh * MEM_HEAD_DIM, (hh + 1) * MEM_HEAD_DIM)
        k = kv_ref[:, sl]
        v = kv_ref[:, MEM_W + hh * MEM_HEAD_DIM:MEM_W + (hh + 1) * MEM_HEAD_DIM]
        s = _dot_nt(q[:, sl], k)
        p = jnp.exp(s - jnp.max(s, axis=-1, keepdims=True))
        p = p / jnp.sum(p, axis=-1, keepdims=True)
        outs.append(_dot(p.astype(BF16), v))
    o_att = jnp.concatenate(outs, axis=1).astype(BF16)
    y = ALPHA * x_ref[...] + _dot(o_att, wo_ref[...])
    o_ref[...] = _layer_norm(y, g_ref[...], b_ref[...])


def _cross(x2, x2b, wq, kv3, wo, g, b, *, seq, tm=512):
    m, d = x2.shape
    tm = min(tm, seq)
    t_per_seq = seq // tm
    mem_len = kv3.shape[1]
    return pl.pallas_call(
        _cross_kernel,
        grid=(m // tm,),
        in_specs=[
            pl.BlockSpec((tm, d), lambda i: (i, 0)),
            pl.BlockSpec((tm, d), lambda i: (i, 0)),
            pl.BlockSpec((d, MEM_W), lambda i: (0, 0)),
            pl.BlockSpec((None, mem_len, 2 * MEM_W), lambda i: (i // t_per_seq, 0, 0)),
            pl.BlockSpec((MEM_W, d), lambda i: (0, 0)),
            pl.BlockSpec((1, d), lambda i: (0, 0)),
            pl.BlockSpec((1, d), lambda i: (0, 0)),
        ],
        out_specs=pl.BlockSpec((tm, d), lambda i: (i, 0)),
        out_shape=jax.ShapeDtypeStruct((m, d), F32),
        compiler_params=_cparams(("parallel",)),
        name="cross_attn",
    )(x2, x2b, wq, kv3, wo, g, b)


def _rope_tables(seq):
    t = np.arange(seq)
    inv_freq = ROPE_THETA ** (-np.arange(ROPE_AXIS_DIM // 2, dtype=np.float64) * 2.0 / ROPE_AXIS_DIM)
    ang_r = (t // GRID_W)[:, None] * inv_freq[None, :]
    ang_c = (t % GRID_W)[:, None] * inv_freq[None, :]
    cr, sr, cc, sc = np.cos(ang_r), np.sin(ang_r), np.cos(ang_c), np.sin(ang_c)
    zr = np.zeros_like(sr)
    rc = np.concatenate([cr, cr, cc, cc], axis=1)
    ra = np.concatenate([-sr, zr, -sc, zr], axis=1)
    rb = np.concatenate([zr, sr, zr, sc], axis=1)
    return tuple(jnp.asarray(tab.astype(np.float32)) for tab in (rc, ra, rb))


def kernel(x, mem, ffn1_w1, ffn1_w3, ffn1_w2, ln1_g, ln1_b, w_in, q_norm_g, k_norm_g, conv_w, conv_b,
           dt_bias, a_log, ssd_d, ssd_norm_g, w_branch_attn, w_branch_ssd, w_mix_out, ln2_g, ln2_b,
           mem_wq, mem_wk, mem_wv, mem_wo, ln3_g, ln3_b, ffn2_w1, ffn2_w3, ffn2_w2, ln4_g, ln4_b):
    bsz, seq, d = x.shape
    m = bsz * seq
    mem_len = mem.shape[1]
    bf = lambda t: t.astype(BF16)
    rc, ra, rb = _rope_tables(seq)
    xc = x.reshape(m, d)
    for l in range(DEPTH):
        row = lambda v: v[l].reshape(1, -1)
        act = _glu(xc, bf(ffn1_w1[l]), bf(ffn1_w3[l]))
        x1, x1b = _proj_ln(act, xc, bf(ffn1_w2[l]), row(ln1_g), row(ln1_b), scale=0.5, tm=256)
        dt0 = Q_W + 2 * KV_W + D_INNER + XBC_W
        w_front = bf(w_in[l][:, :dt0 + DT_W])
        w_gate = bf(w_in[l][:, dt0 + DT_W:])
        gains = jnp.concatenate([jnp.tile(q_norm_g[l] * (HEAD_DIM ** -0.5 * LOG2E), N_Q_HEADS),
                                 jnp.tile(k_norm_g[l], N_KV_HEADS)]).reshape(1, -1)
        h = _in_proj(x1b, w_front, gains, rc, ra, rb, seq=seq)
        dt_raw = _matmul(x1b, w_front, F32, tm=1024, tn=DT_W, n_out=DT_W, col_blk0=dt0 // DT_W)
        gate = _matmul(x1b, w_gate, BF16, tm=2048, tn=COL_BLK)
        h3 = h.reshape(bsz, seq, H_W)
        y_attn = _attention(h3)
        xbc = _conv_silu(h3, conv_w[l], conv_b[l].reshape(1, -1))
        y_ssd = _ssd(xbc, dt_raw.reshape(bsz, seq, DT_W), dt_bias[l].reshape(1, DT_W),
                     a_log[l].reshape(1, DT_W), h3,
                     jnp.repeat(ssd_d[l], SSD_HEAD_DIM).reshape(1, D_INNER),
                     ssd_norm_g[l].reshape(1, D_INNER))
        merged = _branch_mix(y_attn.reshape(m, Q_W), y_ssd.reshape(m, D_INNER), gate,
                             bf(w_branch_attn[l]), bf(w_branch_ssd[l]))
        x2, x2b = _proj_ln(merged, x1, bf(w_mix_out[l]), row(ln2_g), row(ln2_b))
        w_kv = bf(jnp.concatenate([mem_wk[l], mem_wv[l]], axis=1))
        kv = _matmul(bf(mem.reshape(bsz * mem_len, d)), w_kv, BF16, tm=bsz * mem_len, tn=COL_BLK)
        x3 = _cross(x2, x2b, bf(mem_wq[l]), kv.reshape(bsz, mem_len, 2 * MEM_W), bf(mem_wo[l]),
                       row(ln3_g), row(ln3_b), seq=seq)
        act = _glu(x3, bf(ffn2_w1[l]), bf(ffn2_w3[l]))
        (xc,) = _proj_ln(act, x3, bf(ffn2_w2[l]), row(ln4_g), row(ln4_b), scale=0.5, tm=256,
                         emit_bf16=False)
    return xc.reshape(bsz, seq, d)
```

```python
import functools
import math

import jax
import jax.numpy as jnp
import numpy as np
from jax import lax
from jax.experimental import pallas as pl
from jax.experimental.pallas import tpu as pltpu

F32 = jnp.float32
BF16 = jnp.bfloat16

D_MODEL = 2048
DEPTH = 1
GRID_W = 64
HEAD_DIM = 128
N_Q_HEADS = 16
N_KV_HEADS = 4
GQA_REP = 4
ROPE_AXIS_DIM = 64
ROPE_THETA = 10000.0
D_INNER = 4096
SSD_HEAD_DIM = 64
N_SSD_HEADS = 64
N_SSD_GROUPS = 8
HEADS_PER_GROUP = N_SSD_HEADS // N_SSD_GROUPS
GROUP_W = D_INNER // N_SSD_GROUPS
D_STATE = 128
CONV_W = 5
CHUNK = 128
N_MEM_HEADS = 4
MEM_HEAD_DIM = 128
D_FF = 5632
LN_EPS = 1e-5
RMS_EPS = 1e-6
Q_W = 2048
KV_W = 512
BC_W = 1024
XBC_W = 6144
DT_W = 128
MEM_W = 512
ALPHA = (2.0 * DEPTH) ** 0.25
LOG2E = math.log2(math.e)

LANE = 128
COL_BLK = 512
SRC_ROPE_BLKS = (Q_W + KV_W) // COL_BLK
Z_BLK0 = 0
XBC_BLK0 = Z_BLK0 + D_INNER // COL_BLK
V_BLK = XBC_BLK0 + XBC_W // COL_BLK
H_BLKS = V_BLK + 1
H_W = H_BLKS * COL_BLK

VMEM_LIMIT = 56 * 1024 * 1024


def _cparams(sem):
    return pltpu.CompilerParams(dimension_semantics=sem, vmem_limit_bytes=VMEM_LIMIT)


def _dot(a, b):
    return jnp.dot(a, b, preferred_element_type=F32)


def _dot_nt(a, b):
    return lax.dot_general(a, b, (((1,), (1,)), ((), ())), preferred_element_type=F32)


def _dot_tn(a, b):
    return lax.dot_general(a, b, (((0,), (0,)), ((), ())), preferred_element_type=F32)


def _split3(v):
    hi = v.astype(BF16)
    r1 = v - hi.astype(F32)
    mid = r1.astype(BF16)
    lo = (r1 - mid.astype(F32)).astype(BF16)
    return hi, mid, lo


def _layer_norm(y, g, b):
    mu = jnp.mean(y, axis=-1, keepdims=True)
    d = y - mu
    var = jnp.mean(d * d, axis=-1, keepdims=True)
    return d * lax.rsqrt(var + LN_EPS) * g + b


def _silu(x):
    return x * jax.nn.sigmoid(x)


def _glu_kernel(x_ref, w1_ref, w3_ref, o_ref, xb_ref):
    @pl.when(pl.program_id(1) == 0)
    def _():
        xb_ref[...] = x_ref[...].astype(BF16)

    xb = xb_ref[...]
    o_ref[...] = (_silu(_dot(xb, w1_ref[...])) * _dot(xb, w3_ref[...])).astype(BF16)


def _glu(x, w1, w3, *, tm=1024, tf=512):
    m, d = x.shape
    dff = w1.shape[1]
    tm = min(tm, m)
    return pl.pallas_call(
        _glu_kernel,
        grid=(m // tm, dff // tf),
        in_specs=[
            pl.BlockSpec((tm, d), lambda i, f: (i, 0)),
            pl.BlockSpec((d, tf), lambda i, f: (0, f)),
            pl.BlockSpec((d, tf), lambda i, f: (0, f)),
        ],
        out_specs=pl.BlockSpec((tm, tf), lambda i, f: (i, f)),
        out_shape=jax.ShapeDtypeStruct((m, dff), BF16),
        scratch_shapes=[pltpu.VMEM((tm, d), BF16)],
        compiler_params=_cparams(("parallel", "arbitrary")),
        name="glu",
    )(x, w1, w3)


def _qk_proj_kernel(x_ref, w_ref, gain_ref, rc_ref, rs_ref, swap_ref, o_ref):
    acc = _dot(x_ref[...], w_ref[...])
    rc = rc_ref[...]
    rs = rs_ref[...]
    swap = swap_ref[...]
    for hh in range(COL_BLK // HEAD_DIM):
        sl = slice(hh * HEAD_DIM, (hh + 1) * HEAD_DIM)
        xh = acc[:, sl]
        ms = jnp.mean(xh * xh, axis=-1, keepdims=True)
        xh = xh * lax.rsqrt(ms + RMS_EPS) * gain_ref[:, sl]
        hi = xh.astype(BF16)
        mid = (xh - hi.astype(F32)).astype(BF16)
        partner = _dot(jnp.concatenate([hi, mid], axis=1), swap)
        o_ref[:, sl] = (xh * rc + partner * rs).astype(BF16)


def _qk_proj(xb, w, gains, rc, rs, swap, *, seq, tm=2048):
    m, d = xb.shape
    tm = min(tm, seq)
    t_per_seq = seq // tm
    return pl.pallas_call(
        _qk_proj_kernel,
        grid=(m // tm, SRC_ROPE_BLKS),
        in_specs=[
            pl.BlockSpec((tm, d), lambda i, n: (i, 0)),
            pl.BlockSpec((d, COL_BLK), lambda i, n: (0, n)),
            pl.BlockSpec((1, COL_BLK), lambda i, n: (0, n)),
            pl.BlockSpec((tm, HEAD_DIM), lambda i, n: (i % t_per_seq, 0)),
            pl.BlockSpec((tm, HEAD_DIM), lambda i, n: (i % t_per_seq, 0)),
            pl.BlockSpec((2 * HEAD_DIM, HEAD_DIM), lambda i, n: (0, 0)),
        ],
        out_specs=pl.BlockSpec((tm, COL_BLK), lambda i, n: (i, n)),
        out_shape=jax.ShapeDtypeStruct((m, SRC_ROPE_BLKS * COL_BLK), BF16),
        compiler_params=_cparams(("parallel", "arbitrary")),
        name="qk_proj",
    )(xb, w, gains, rc, rs, swap)


def _matmul_kernel(x_ref, *refs):
    *w_refs, o_ref = refs
    w = w_refs[0][...] if len(w_refs) == 1 else jnp.concatenate([r[...] for r in w_refs], axis=1)
    o_ref[...] = _dot(x_ref[...], w).astype(o_ref.dtype)


def _matmul(xb, w, out_dtype, *, tm, tn, n_out=None, col0=0, out_rot=0, strips=1):
    m, d = xb.shape
    n = w.shape[1] if n_out is None else n_out
    tm = min(tm, m)
    n_blk = n // tn
    ts = tn // strips
    s0 = col0 // ts
    w_spec = lambda s: pl.BlockSpec((d, ts), lambda i, j: (0, s0 + strips * j + s))
    return pl.pallas_call(
        _matmul_kernel,
        grid=(m // tm, n_blk),
        in_specs=[pl.BlockSpec((tm, d), lambda i, j: (i, 0))] + [w_spec(s) for s in range(strips)],
        out_specs=pl.BlockSpec((tm, tn), lambda i, j: (i, (j + n_blk - out_rot) % n_blk)),
        out_shape=jax.ShapeDtypeStruct((m, n), out_dtype),
        compiler_params=_cparams(("parallel", "arbitrary")),
        name="matmul",
    )(xb, *([w] * strips))


SCORE_SAFE_LOG2 = 60.0


def _attn_kernel(safe_ref, q_ref, k_ref, v_ref, o_ref, v1_ref, p_ref, *, tq, tk):
    seq = k_ref.shape[0]

    @pl.when(pl.program_id(2) == 0)
    def _():
        v1_ref[:, :HEAD_DIM] = v_ref[...]
        v1_ref[:, HEAD_DIM:] = jnp.ones((seq, HEAD_DIM), BF16)

    q = jnp.concatenate([q_ref[:, hh * HEAD_DIM:(hh + 1) * HEAD_DIM] for hh in range(GQA_REP)], axis=0)
    rows = GQA_REP * tq

    def finish(acc):
        o = (acc[:, :HEAD_DIM] / acc[:, HEAD_DIM:HEAD_DIM + 1]).astype(BF16)
        for hh in range(GQA_REP):
            o_ref[:, hh * HEAD_DIM:(hh + 1) * HEAD_DIM] = o[hh * tq:(hh + 1) * tq]

    @pl.when(safe_ref[0] == 1)
    def _():
        for j in range(seq // tk):
            s = _dot_nt(q, k_ref[j * tk:(j + 1) * tk, :])
            p_ref[:, j * tk:(j + 1) * tk] = jnp.exp2(s).astype(BF16)
        finish(_dot(p_ref[...], v1_ref[...]))

    @pl.when(safe_ref[0] == 0)
    def _():
        m_i = jnp.full((rows, 1), -jnp.inf, F32)
        acc = jnp.zeros((rows, 2 * HEAD_DIM), F32)
        for j in range(seq // tk):
            s = _dot_nt(q, k_ref[j * tk:(j + 1) * tk, :])
            m_new = jnp.maximum(m_i, jnp.max(s, axis=-1, keepdims=True))
            p = jnp.exp2(s - m_new).astype(BF16)
            acc = jnp.exp2(m_i - m_new) * acc + _dot(p, v1_ref[j * tk:(j + 1) * tk, :])
            m_i = m_new
        finish(acc)


def _attention(qk3, h3, scores_safe, *, tq=256, tk=512):
    bsz, seq, _ = h3.shape
    tk = min(tk, seq)
    k_blk0 = Q_W // HEAD_DIM
    v_blk0 = V_BLK * COL_BLK // HEAD_DIM
    return pl.pallas_call(
        functools.partial(_attn_kernel, tq=tq, tk=tk),
        grid=(bsz, N_KV_HEADS, seq // tq),
        in_specs=[
            pl.BlockSpec(memory_space=pltpu.SMEM),
            pl.BlockSpec((None, tq, GQA_REP * HEAD_DIM), lambda b, kv, i: (b, i, kv)),
            pl.BlockSpec((None, seq, HEAD_DIM), lambda b, kv, i: (b, 0, k_blk0 + kv)),
            pl.BlockSpec((None, seq, HEAD_DIM), lambda b, kv, i: (b, 0, v_blk0 + kv)),
        ],
        out_specs=pl.BlockSpec((None, tq, GQA_REP * HEAD_DIM), lambda b, kv, i: (b, i, kv)),
        out_shape=jax.ShapeDtypeStruct((bsz, seq, Q_W), BF16),
        scratch_shapes=[pltpu.VMEM((seq, 2 * HEAD_DIM), BF16),
                        pltpu.VMEM((GQA_REP * tq, seq), BF16)],
        compiler_params=_cparams(("parallel", "parallel", "arbitrary")),
        name="attention",
    )(scores_safe, qk3, qk3, h3)


CONV_HALO = 8
CONV_SUB = 128
CONV_PAD = CONV_W // 2


def _conv_kernel(prev_ref, main_ref, next_ref, w_ref, b_ref, o_ref, win_ref, *, rt):
    r = pl.program_id(1)
    first = r == 0
    last = r == pl.num_programs(1) - 1
    win_ref[0:CONV_HALO, :] = jnp.where(first, 0.0, prev_ref[...].astype(F32))
    win_ref[CONV_HALO:CONV_HALO + rt, :] = main_ref[...].astype(F32)
    win_ref[CONV_HALO + rt:, :] = jnp.where(last, 0.0, next_ref[...].astype(F32))
    bias = b_ref[...]
    span = CONV_SUB + 2 * CONV_HALO
    for t in range(rt // CONV_SUB):
        win = win_ref[t * CONV_SUB:t * CONV_SUB + span, :]
        acc = bias + w_ref[CONV_PAD:CONV_PAD + 1, :] * win[CONV_HALO:CONV_HALO + CONV_SUB]
        for k in range(CONV_W):
            if k == CONV_PAD:
                continue
            rolled = pltpu.roll(win, (CONV_PAD - k) % span, 0)
            acc = acc + w_ref[k:k + 1, :] * rolled[CONV_HALO:CONV_HALO + CONV_SUB]
        o_ref[t * CONV_SUB:(t + 1) * CONV_SUB, :] = _silu(acc).astype(BF16)


def _conv_silu(h3, conv_w, conv_b, *, rt=1024):
    bsz, seq, _ = h3.shape
    rt = min(rt, seq)
    nh = rt // CONV_HALO
    n_halo_blk = seq // CONV_HALO
    return pl.pallas_call(
        functools.partial(_conv_kernel, rt=rt),
        grid=(bsz, seq // rt, XBC_W // COL_BLK),
        in_specs=[
            pl.BlockSpec((None, CONV_HALO, COL_BLK),
                         lambda b, r, c: (b, jnp.maximum(r * nh - 1, 0), XBC_BLK0 + c)),
            pl.BlockSpec((None, rt, COL_BLK), lambda b, r, c: (b, r, XBC_BLK0 + c)),
            pl.BlockSpec((None, CONV_HALO, COL_BLK),
                         lambda b, r, c: (b, jnp.minimum((r + 1) * nh, n_halo_blk - 1), XBC_BLK0 + c)),
            pl.BlockSpec((CONV_W, COL_BLK), lambda b, r, c: (0, c)),
            pl.BlockSpec((1, COL_BLK), lambda b, r, c: (0, c)),
        ],
        out_specs=pl.BlockSpec((None, rt, COL_BLK), lambda b, r, c: (b, r, c)),
        out_shape=jax.ShapeDtypeStruct((bsz, seq, XBC_W), BF16),
        scratch_shapes=[pltpu.VMEM((rt + 2 * CONV_HALO, COL_BLK), F32)],
        compiler_params=_cparams(("parallel", "parallel", "arbitrary")),
        name="conv_silu",
    )(h3, h3, h3, conv_w, conv_b)


def _softplus(x):
    return jnp.maximum(x, 0.0) + jnp.log1p(jnp.exp(-jnp.abs(x)))


def _cumsum_pos(t01, a):
    hi, mid, lo = _split3(a)
    t2 = jnp.concatenate([t01, t01], axis=1)
    return _dot(t2, jnp.concatenate([hi, mid], axis=0)) + _dot(t01, lo)


def _ssd_kernel(*refs, backward):
    if backward:
        (xs_ref, bm_ref, cm_ref, dt_ref, bias_ref, alog_ref, ex_ref, yf_ref, z_ref, dvec_ref, ng_ref,
         o_ref, h_ref) = refs
    else:
        xs_ref, bm_ref, cm_ref, dt_ref, bias_ref, alog_ref, ex_ref, o_ref, h_ref = refs

    @pl.when(pl.program_id(1) == 0)
    def _():
        h_ref[...] = jnp.zeros_like(h_ref)

    lane = lax.broadcasted_iota(jnp.int32, (CHUNK, LANE), 1)
    row = lax.broadcasted_iota(jnp.int32, (CHUNK, LANE), 0)
    nh = N_SSD_HEADS
    hpg = HEADS_PER_GROUP

    dt = _softplus(dt_ref[...] + bias_ref[...])
    a = dt * (-LOG2E * jnp.exp(alog_ref[...]))
    tri_incl = lane <= row
    t_lo = jnp.where(tri_incl, 1.0, 0.0).astype(BF16)
    t_up = jnp.where(lane >= row, 1.0, 0.0).astype(BF16)
    cs = jnp.where(lane < nh, _cumsum_pos(t_lo, a), _cumsum_pos(t_up, a))

    off = nh if backward else 0
    if backward:
        edge = cs[0:1, :]
    else:
        edge = cs[CHUNK - 1:CHUNK, :]
    used = jnp.logical_and(lane >= off, lane < off + nh)
    e_pos = jnp.where(used, jnp.exp2(jnp.where(used, cs, 0.0)), 0.0)
    w_pos = jnp.where(used, jnp.exp2(jnp.where(used, edge - cs, 0.0)) * dt, 0.0)
    st = jnp.concatenate([e_pos, w_pos], axis=0)
    st_hi = st.astype(BF16)
    st_mid = (st - st_hi.astype(F32)).astype(BF16)
    st2 = jnp.concatenate([st_hi, st_mid], axis=1)
    if not backward:
        src_t = (cs - LOG2E * jnp.log(dt)).T

    for g in range(N_SSD_GROUPS):
        sl = slice(g * GROUP_W, (g + 1) * GROUP_W)
        sb = slice(g * D_STATE, (g + 1) * D_STATE)
        xs_f = xs_ref[:, sl].astype(F32)
        bm = bm_ref[:, sb]
        cm = cm_ref[:, sb]
        ex = _dot(st2, ex_ref[g])
        e_exp = ex[:CHUNK]
        w_exp = ex[CHUNK:]
        chunk_decay = e_exp[0:1, :] if backward else e_exp[CHUNK - 1:CHUNK, :]
        h = h_ref[g]
        y_off = _dot(cm, h.astype(BF16)) * e_exp
        s_t = _dot_tn(bm, (xs_f * w_exp).astype(BF16))
        h_ref[g] = h * chunk_decay + s_t

        if not backward:
            cb = _dot_nt(cm, bm)
            pairs = []
            for pp in range(hpg // 2):
                x_pair = xs_f[:, pp * LANE:(pp + 1) * LANE]
                ms = []
                for hh in (g * hpg + 2 * pp, g * hpg + 2 * pp + 1):
                    arg = jnp.where(tri_incl,
                                    cs[:, hh:hh + 1] - src_t[hh:hh + 1, :],
                                    cs[:, nh + hh:nh + hh + 1] - src_t[nh + hh:nh + hh + 1, :])
                    ms.append((cb * jnp.exp2(arg)).astype(BF16))
                x_a = jnp.where(lane < SSD_HEAD_DIM, x_pair, 0.0).astype(BF16)
                x_b = jnp.where(lane >= SSD_HEAD_DIM, x_pair, 0.0).astype(BF16)
                pairs.append(_dot(jnp.concatenate(ms, axis=1), jnp.concatenate([x_a, x_b], axis=0)))
            o_ref[:, sl] = jnp.concatenate(pairs, axis=1) + y_off
        else:
            y = yf_ref[:, sl] + y_off + dvec_ref[:, sl] * xs_f
            y = y * _silu(z_ref[:, sl].astype(F32))
            y = y * lax.rsqrt(jnp.mean(y * y, axis=-1, keepdims=True) + RMS_EPS) * ng_ref[:, sl]
            o_ref[:, sl] = y.astype(BF16)


def _expand_mats(off):
    k = (np.arange(2 * LANE) % LANE)[None, :, None]
    col = (np.arange(GROUP_W) // SSD_HEAD_DIM)[None, None, :]
    g = np.arange(N_SSD_GROUPS)[:, None, None]
    return jnp.asarray(k == off + HEADS_PER_GROUP * g + col, dtype=BF16)


def _ssd(xbc3, dt3, bias, alog, h3, dvec, ng):
    bsz, seq, _ = xbc3.shape
    nc = seq // CHUNK
    g_n = N_SSD_GROUPS
    scratch = [pltpu.VMEM((g_n, D_STATE, GROUP_W), F32)]
    sem = ("parallel", "arbitrary")

    def common_specs(cidx):
        return [
            pl.BlockSpec((None, CHUNK, D_INNER), lambda b, j: (b, cidx(j), 0)),
            pl.BlockSpec((None, CHUNK, BC_W), lambda b, j: (b, cidx(j), D_INNER // BC_W)),
            pl.BlockSpec((None, CHUNK, BC_W), lambda b, j: (b, cidx(j), D_INNER // BC_W + 1)),
            pl.BlockSpec((None, CHUNK, LANE), lambda b, j: (b, cidx(j), 0)),
            pl.BlockSpec((1, LANE), lambda b, j: (0, 0)),
            pl.BlockSpec((1, LANE), lambda b, j: (0, 0)),
            pl.BlockSpec((g_n, 2 * LANE, GROUP_W), lambda b, j: (0, 0, 0)),
        ]

    fwd = lambda j: j
    y_f = pl.pallas_call(
        functools.partial(_ssd_kernel, backward=False),
        grid=(bsz, nc),
        in_specs=common_specs(fwd),
        out_specs=pl.BlockSpec((None, CHUNK, D_INNER), lambda b, j: (b, j, 0)),
        out_shape=jax.ShapeDtypeStruct((bsz, seq, D_INNER), F32),
        scratch_shapes=scratch,
        compiler_params=_cparams(sem),
        name="ssd_fwd",
    )(xbc3, xbc3, xbc3, dt3, bias, alog, _expand_mats(0))

    bwd = lambda j: nc - 1 - j
    return pl.pallas_call(
        functools.partial(_ssd_kernel, backward=True),
        grid=(bsz, nc),
        in_specs=common_specs(bwd) + [
            pl.BlockSpec((None, CHUNK, D_INNER), lambda b, j: (b, bwd(j), 0)),
            pl.BlockSpec((None, CHUNK, D_INNER), lambda b, j: (b, bwd(j), Z_BLK0 * COL_BLK // D_INNER)),
            pl.BlockSpec((1, D_INNER), lambda b, j: (0, 0)),
            pl.BlockSpec((1, D_INNER), lambda b, j: (0, 0)),
        ],
        out_specs=pl.BlockSpec((None, CHUNK, D_INNER), lambda b, j: (b, bwd(j), 0)),
        out_shape=jax.ShapeDtypeStruct((bsz, seq, D_INNER), BF16),
        scratch_shapes=scratch,
        compiler_params=_cparams(sem),
        name="ssd_bwd",
    )(xbc3, xbc3, xbc3, dt3, bias, alog, _expand_mats(N_SSD_HEADS), y_f, h3, dvec, ng)


def _branch_mix_kernel(ya_ref, ys_ref, ga_ref, gs_ref, wa_ref, ws_ref, o_ref):
    pa = jax.nn.sigmoid(ga_ref[...].astype(F32)) * _dot(ya_ref[...], wa_ref[...])
    ps = jax.nn.sigmoid(gs_ref[...].astype(F32)) * _dot(ys_ref[...], ws_ref[...])
    o_ref[...] = (pa + ps).astype(BF16)


def _branch_mix(y_attn, y_ssd, gate, wa, ws, *, tm=1024, tn=512):
    m = y_attn.shape[0]
    d = wa.shape[1]
    tm = min(tm, m)
    n_blk = d // tn
    return pl.pallas_call(
        _branch_mix_kernel,
        grid=(m // tm, n_blk),
        in_specs=[
            pl.BlockSpec((tm, Q_W), lambda i, n: (i, 0)),
            pl.BlockSpec((tm, D_INNER), lambda i, n: (i, 0)),
            pl.BlockSpec((tm, tn), lambda i, n: (i, n)),
            pl.BlockSpec((tm, tn), lambda i, n: (i, n_blk + n)),
            pl.BlockSpec((Q_W, tn), lambda i, n: (0, n)),
            pl.BlockSpec((D_INNER, tn), lambda i, n: (0, n)),
        ],
        out_specs=pl.BlockSpec((tm, tn), lambda i, n: (i, n)),
        out_shape=jax.ShapeDtypeStruct((m, d), BF16),
        compiler_params=_cparams(("parallel", "arbitrary")),
        name="branch_mix",
    )(y_attn, y_ssd, gate, gate, wa, ws)


def _proj_ln_kernel(a_ref, x_ref, w_ref, g_ref, b_ref, o_ref, *maybe_ob_ref, scale):
    y = _dot(a_ref[...], w_ref[...])
    if scale != 1.0:
        y = scale * y
    o = _layer_norm(ALPHA * x_ref[...] + y, g_ref[...], b_ref[...])
    o_ref[...] = o
    for ob_ref in maybe_ob_ref:
        ob_ref[...] = o.astype(BF16)


def _proj_ln(a, x, w, g, b, *, scale=1.0, tm=512, emit_bf16=True):
    m, d = x.shape
    k = a.shape[1]
    tm = min(tm, m)
    n_out = 2 if emit_bf16 else 1
    return pl.pallas_call(
        functools.partial(_proj_ln_kernel, scale=scale),
        grid=(m // tm,),
        in_specs=[
            pl.BlockSpec((tm, k), lambda i: (i, 0)),
            pl.BlockSpec((tm, d), lambda i: (i, 0)),
            pl.BlockSpec((k, d), lambda i: (0, 0), pipeline_mode=pl.Buffered(1)),
            pl.BlockSpec((1, d), lambda i: (0, 0)),
            pl.BlockSpec((1, d), lambda i: (0, 0)),
        ],
        out_specs=[pl.BlockSpec((tm, d), lambda i: (i, 0))] * n_out,
        out_shape=[jax.ShapeDtypeStruct((m, d), F32), jax.ShapeDtypeStruct((m, d), BF16)][:n_out],
        compiler_params=_cparams(("parallel",)),
        name="proj_ln",
    )(a, x, w, g, b)


def _cross_kernel(x_ref, xb_ref, wq_ref, kv_ref, wo_ref, g_ref, b_ref, o_ref):
    scale = MEM_HEAD_DIM ** -0.5
    q = (_dot(xb_ref[...], wq_ref[...]) * scale).astype(BF16)
    outs = []
    for hh in range(N_MEM_HEADS):
        sl = slice(hh * MEM_HEAD_DIM, (hh + 1) * MEM_HEAD_DIM)
        k = kv_ref[:, sl]
        v = kv_ref[:, MEM_W + hh * MEM_HEAD_DIM:MEM_W + (hh + 1) * MEM_HEAD_DIM]
        s = _dot_nt(q[:, sl], k)
        p = jnp.exp(s - jnp.max(s, axis=-1, keepdims=True))
        p = p / jnp.sum(p, axis=-1, keepdims=True)
        outs.append(_dot(p.astype(BF16), v))
    o_att = jnp.concatenate(outs, axis=1).astype(BF16)
    y = ALPHA * x_ref[...] + _dot(o_att, wo_ref[...])
    o_ref[...] = _layer_norm(y, g_ref[...], b_ref[...])


def _cross(x2, x2b, wq, kv3, wo, g, b, *, seq, tm=512):
    m, d = x2.shape
    tm = min(tm, seq)
    t_per_seq = seq // tm
    mem_len = kv3.shape[1]
    return pl.pallas_call(
        _cross_kernel,
        grid=(m // tm,),
        in_specs=[
            pl.BlockSpec((tm, d), lambda i: (i, 0)),
            pl.BlockSpec((tm, d), lambda i: (i, 0)),
            pl.BlockSpec((d, MEM_W), lambda i: (0, 0)),
            pl.BlockSpec((None, mem_len, 2 * MEM_W), lambda i: (i // t_per_seq, 0, 0)),
            pl.BlockSpec((MEM_W, d), lambda i: (0, 0)),
            pl.BlockSpec((1, d), lambda i: (0, 0)),
            pl.BlockSpec((1, d), lambda i: (0, 0)),
        ],
        out_specs=pl.BlockSpec((tm, d), lambda i: (i, 0)),
        out_shape=jax.ShapeDtypeStruct((m, d), F32),
        compiler_params=_cparams(("parallel",)),
        name="cross_attn",
    )(x2, x2b, wq, kv3, wo, g, b)


def _rope_tables(seq):
    t = np.arange(seq)
    inv_freq = ROPE_THETA ** (-np.arange(ROPE_AXIS_DIM // 2, dtype=np.float64) * 2.0 / ROPE_AXIS_DIM)
    ang_r = (t // GRID_W)[:, None] * inv_freq[None, :]
    ang_c = (t % GRID_W)[:, None] * inv_freq[None, :]
    cr, sr, cc, sc = np.cos(ang_r), np.sin(ang_r), np.cos(ang_c), np.sin(ang_c)
    rc = np.concatenate([cr, cr, cc, cc], axis=1)
    rs = np.concatenate([-sr, sr, -sc, sc], axis=1)
    lane = np.arange(HEAD_DIM)
    swap = (lane[:, None] == (lane[None, :] ^ (ROPE_AXIS_DIM // 2)))
    swap2 = np.concatenate([swap, swap], axis=0)
    return (jnp.asarray(rc.astype(np.float32)), jnp.asarray(rs.astype(np.float32)),
            jnp.asarray(swap2, dtype=BF16))


def kernel(x, mem, ffn1_w1, ffn1_w3, ffn1_w2, ln1_g, ln1_b, w_in, q_norm_g, k_norm_g, conv_w, conv_b,
           dt_bias, a_log, ssd_d, ssd_norm_g, w_branch_attn, w_branch_ssd, w_mix_out, ln2_g, ln2_b,
           mem_wq, mem_wk, mem_wv, mem_wo, ln3_g, ln3_b, ffn2_w1, ffn2_w3, ffn2_w2, ln4_g, ln4_b):
    bsz, seq, d = x.shape
    m = bsz * seq
    mem_len = mem.shape[1]
    bf = lambda t: t.astype(BF16)
    rc, rs, swap = _rope_tables(seq)
    xc = x.reshape(m, d)
    for l in range(DEPTH):
        row = lambda v: v[l].reshape(1, -1)
        act = _glu(xc, bf(ffn1_w1[l]), bf(ffn1_w3[l]))
        x1, x1b = _proj_ln(act, xc, bf(ffn1_w2[l]), row(ln1_g), row(ln1_b), scale=0.5, tm=256)
        dt0 = Q_W + 2 * KV_W + D_INNER + XBC_W
        w_all = bf(w_in[l])
        gains = jnp.concatenate([jnp.tile(q_norm_g[l] * (HEAD_DIM ** -0.5 * LOG2E), N_Q_HEADS),
                                 jnp.tile(k_norm_g[l], N_KV_HEADS)]).reshape(1, -1)
        qk = _qk_proj(x1b, w_all, gains, rc, rs, swap, seq=seq)
        h = _matmul(x1b, w_all, BF16, tm=2048, tn=COL_BLK, n_out=H_W, col0=SRC_ROPE_BLKS * COL_BLK,
                    out_rot=1)
        dt_raw = _matmul(x1b, w_all, F32, tm=1024, tn=DT_W, n_out=DT_W, col0=dt0)
        gate = _matmul(x1b, w_all, BF16, tm=2048, tn=COL_BLK, n_out=2 * D_MODEL, col0=dt0 + DT_W,
                       strips=COL_BLK // LANE)
        h3 = h.reshape(bsz, seq, H_W)
        score_bound = HEAD_DIM * jnp.max(jnp.abs(gains[0, :Q_W])) * jnp.max(jnp.abs(gains[0, Q_W:]))
        y_attn = _attention(qk.reshape(bsz, seq, SRC_ROPE_BLKS * COL_BLK), h3,
                            (score_bound < SCORE_SAFE_LOG2).astype(jnp.int32).reshape(1))
        xbc = _conv_silu(h3, conv_w[l], conv_b[l].reshape(1, -1))
        y_ssd = _ssd(xbc, dt_raw.reshape(bsz, seq, DT_W), dt_bias[l].reshape(1, DT_W),
                     a_log[l].reshape(1, DT_W), h3,
                     jnp.repeat(ssd_d[l], SSD_HEAD_DIM).reshape(1, D_INNER),
                     ssd_norm_g[l].reshape(1, D_INNER))
        merged = _branch_mix(y_attn.reshape(m, Q_W), y_ssd.reshape(m, D_INNER), gate,
                             bf(w_branch_attn[l]), bf(w_branch_ssd[l]))
        x2, x2b = _proj_ln(merged, x1, bf(w_mix_out[l]), row(ln2_g), row(ln2_b))
        w_kv = bf(jnp.concatenate([mem_wk[l], mem_wv[l]], axis=1))
        kv = _matmul(bf(mem.reshape(bsz * mem_len, d)), w_kv, BF16, tm=bsz * mem_len, tn=COL_BLK)
        x3 = _cross(x2, x2b, bf(mem_wq[l]), kv.reshape(bsz, mem_len, 2 * MEM_W), bf(mem_wo[l]),
                       row(ln3_g), row(ln3_b), seq=seq)
        act = _glu(x3, bf(ffn2_w1[l]), bf(ffn2_w3[l]))
        (xc,) = _proj_ln(act, x3, bf(ffn2_w2[l]), row(ln4_g), row(ln4_b), scale=0.5, tm=256,
                         emit_bf16=False)
    return xc.reshape(bsz, seq, d)
```

```python
import functools
import math

import jax
import jax.numpy as jnp
import numpy as np
from jax import lax
from jax.experimental import pallas as pl
from jax.experimental.pallas import tpu as pltpu

F32 = jnp.float32
BF16 = jnp.bfloat16

D_MODEL = 2048
DEPTH = 1
GRID_W = 64
HEAD_DIM = 128
N_Q_HEADS = 16
N_KV_HEADS = 4
GQA_REP = 4
ROPE_AXIS_DIM = 64
ROPE_THETA = 10000.0
D_INNER = 4096
SSD_HEAD_DIM = 64
N_SSD_HEADS = 64
N_SSD_GROUPS = 8
HEADS_PER_GROUP = N_SSD_HEADS // N_SSD_GROUPS
GROUP_W = D_INNER // N_SSD_GROUPS
D_STATE = 128
CONV_W = 5
CHUNK = 128
N_MEM_HEADS = 4
MEM_HEAD_DIM = 128
D_FF = 5632
LN_EPS = 1e-5
RMS_EPS = 1e-6
Q_W = 2048
KV_W = 512
BC_W = 1024
XBC_W = 6144
DT_W = 128
MEM_W = 512
ALPHA = (2.0 * DEPTH) ** 0.25
LOG2E = math.log2(math.e)

LANE = 128
COL_BLK = 512
SRC_ROPE_BLKS = (Q_W + KV_W) // COL_BLK
Z_BLK0 = 0
XBC_BLK0 = Z_BLK0 + D_INNER // COL_BLK
V_BLK = XBC_BLK0 + XBC_W // COL_BLK
H_BLKS = V_BLK + 1
H_W = H_BLKS * COL_BLK

VMEM_LIMIT = 56 * 1024 * 1024


def _cparams(sem):
    return pltpu.CompilerParams(dimension_semantics=sem, vmem_limit_bytes=VMEM_LIMIT)


def _dot(a, b):
    return jnp.dot(a, b, preferred_element_type=F32)


def _dot_nt(a, b):
    return lax.dot_general(a, b, (((1,), (1,)), ((), ())), preferred_element_type=F32)


def _dot_tn(a, b):
    return lax.dot_general(a, b, (((0,), (0,)), ((), ())), preferred_element_type=F32)


def _split3(v):
    hi = v.astype(BF16)
    r1 = v - hi.astype(F32)
    mid = r1.astype(BF16)
    lo = (r1 - mid.astype(F32)).astype(BF16)
    return hi, mid, lo


def _layer_norm(y, g, b):
    mu = jnp.mean(y, axis=-1, keepdims=True)
    d = y - mu
    var = jnp.mean(d * d, axis=-1, keepdims=True)
    return d * lax.rsqrt(var + LN_EPS) * g + b


def _silu(x):
    return x * jax.nn.sigmoid(x)


def _glu_kernel(x_ref, w1_ref, w3_ref, o_ref, xb_ref):
    @pl.when(pl.program_id(1) == 0)
    def _():
        xb_ref[...] = x_ref[...].astype(BF16)

    xb = xb_ref[...]
    o_ref[...] = (_silu(_dot(xb, w1_ref[...])) * _dot(xb, w3_ref[...])).astype(BF16)


def _glu(x, w1, w3, *, tm=1024, tf=512):
    m, d = x.shape
    dff = w1.shape[1]
    tm = min(tm, m)
    return pl.pallas_call(
        _glu_kernel,
        grid=(m // tm, dff // tf),
        in_specs=[
            pl.BlockSpec((tm, d), lambda i, f: (i, 0)),
            pl.BlockSpec((d, tf), lambda i, f: (0, f)),
            pl.BlockSpec((d, tf), lambda i, f: (0, f)),
        ],
        out_specs=pl.BlockSpec((tm, tf), lambda i, f: (i, f)),
        out_shape=jax.ShapeDtypeStruct((m, dff), BF16),
        scratch_shapes=[pltpu.VMEM((tm, d), BF16)],
        compiler_params=_cparams(("parallel", "arbitrary")),
        name="glu",
    )(x, w1, w3)


def _qk_proj_kernel(x_ref, w_ref, gain_ref, rc_ref, rs_ref, swap_ref, o_ref):
    acc = _dot(x_ref[...], w_ref[...])
    rc = rc_ref[...]
    rs = rs_ref[...]
    swap = swap_ref[...]
    for hh in range(COL_BLK // HEAD_DIM):
        sl = slice(hh * HEAD_DIM, (hh + 1) * HEAD_DIM)
        xh = acc[:, sl]
        ms = jnp.mean(xh * xh, axis=-1, keepdims=True)
        xh = xh * lax.rsqrt(ms + RMS_EPS) * gain_ref[:, sl]
        hi = xh.astype(BF16)
        mid = (xh - hi.astype(F32)).astype(BF16)
        partner = _dot(jnp.concatenate([hi, mid], axis=1), swap)
        o_ref[:, sl] = (xh * rc + partner * rs).astype(BF16)


def _qk_proj(xb, w, gains, rc, rs, swap, *, seq, tm=2048):
    m, d = xb.shape
    tm = min(tm, seq)
    t_per_seq = seq // tm
    return pl.pallas_call(
        _qk_proj_kernel,
        grid=(m // tm, SRC_ROPE_BLKS),
        in_specs=[
            pl.BlockSpec((tm, d), lambda i, n: (i, 0)),
            pl.BlockSpec((d, COL_BLK), lambda i, n: (0, n)),
            pl.BlockSpec((1, COL_BLK), lambda i, n: (0, n)),
            pl.BlockSpec((tm, HEAD_DIM), lambda i, n: (i % t_per_seq, 0)),
            pl.BlockSpec((tm, HEAD_DIM), lambda i, n: (i % t_per_seq, 0)),
            pl.BlockSpec((2 * HEAD_DIM, HEAD_DIM), lambda i, n: (0, 0)),
        ],
        out_specs=pl.BlockSpec((tm, COL_BLK), lambda i, n: (i, n)),
        out_shape=jax.ShapeDtypeStruct((m, SRC_ROPE_BLKS * COL_BLK), BF16),
        compiler_params=_cparams(("parallel", "arbitrary")),
        name="qk_proj",
    )(xb, w, gains, rc, rs, swap)


def _matmul_kernel(x_ref, *refs):
    *w_refs, o_ref = refs
    w = w_refs[0][...] if len(w_refs) == 1 else jnp.concatenate([r[...] for r in w_refs], axis=1)
    o_ref[...] = _dot(x_ref[...], w).astype(o_ref.dtype)


def _matmul(xb, w, out_dtype, *, tm, tn, n_out=None, col0=0, out_rot=0, strips=1):
    m, d = xb.shape
    n = w.shape[1] if n_out is None else n_out
    tm = min(tm, m)
    n_blk = n // tn
    ts = tn // strips
    s0 = col0 // ts
    w_spec = lambda s: pl.BlockSpec((d, ts), lambda i, j: (0, s0 + strips * j + s))
    return pl.pallas_call(
        _matmul_kernel,
        grid=(m // tm, n_blk),
        in_specs=[pl.BlockSpec((tm, d), lambda i, j: (i, 0))] + [w_spec(s) for s in range(strips)],
        out_specs=pl.BlockSpec((tm, tn), lambda i, j: (i, (j + n_blk - out_rot) % n_blk)),
        out_shape=jax.ShapeDtypeStruct((m, n), out_dtype),
        compiler_params=_cparams(("parallel", "arbitrary")),
        name="matmul",
    )(xb, *([w] * strips))


SCORE_SAFE_LOG2 = 60.0


def _attn_kernel(safe_ref, q_ref, k_ref, v_ref, o_ref, v1_ref, p_ref, *, tq, tk):
    seq = k_ref.shape[0]

    @pl.when(pl.program_id(2) == 0)
    def _():
        v1_ref[:, :HEAD_DIM] = v_ref[...]
        v1_ref[:, HEAD_DIM:] = jnp.ones((seq, HEAD_DIM), BF16)

    q = jnp.concatenate([q_ref[:, hh * HEAD_DIM:(hh + 1) * HEAD_DIM] for hh in range(GQA_REP)], axis=0)
    rows = GQA_REP * tq

    def finish(acc):
        o = (acc[:, :HEAD_DIM] / acc[:, HEAD_DIM:HEAD_DIM + 1]).astype(BF16)
        for hh in range(GQA_REP):
            o_ref[:, hh * HEAD_DIM:(hh + 1) * HEAD_DIM] = o[hh * tq:(hh + 1) * tq]

    @pl.when(safe_ref[0] == 1)
    def _():
        for j in range(seq // tk):
            s = _dot_nt(q, k_ref[j * tk:(j + 1) * tk, :])
            p_ref[:, j * tk:(j + 1) * tk] = jnp.exp2(s).astype(BF16)
        finish(_dot(p_ref[...], v1_ref[...]))

    @pl.when(safe_ref[0] == 0)
    def _():
        m_i = jnp.full((rows, 1), -jnp.inf, F32)
        acc = jnp.zeros((rows, 2 * HEAD_DIM), F32)
        for j in range(seq // tk):
            s = _dot_nt(q, k_ref[j * tk:(j + 1) * tk, :])
            m_new = jnp.maximum(m_i, jnp.max(s, axis=-1, keepdims=True))
            p = jnp.exp2(s - m_new).astype(BF16)
            acc = jnp.exp2(m_i - m_new) * acc + _dot(p, v1_ref[j * tk:(j + 1) * tk, :])
            m_i = m_new
        finish(acc)


def _attention(qk3, h3, scores_safe, *, tq=512, tk=512):
    bsz, seq, _ = h3.shape
    tk = min(tk, seq)
    k_blk0 = Q_W // HEAD_DIM
    v_blk0 = V_BLK * COL_BLK // HEAD_DIM
    return pl.pallas_call(
        functools.partial(_attn_kernel, tq=tq, tk=tk),
        grid=(bsz, N_KV_HEADS, seq // tq),
        in_specs=[
            pl.BlockSpec(memory_space=pltpu.SMEM),
            pl.BlockSpec((None, tq, GQA_REP * HEAD_DIM), lambda b, kv, i: (b, i, kv)),
            pl.BlockSpec((None, seq, HEAD_DIM), lambda b, kv, i: (b, 0, k_blk0 + kv)),
            pl.BlockSpec((None, seq, HEAD_DIM), lambda b, kv, i: (b, 0, v_blk0 + kv)),
        ],
        out_specs=pl.BlockSpec((None, tq, GQA_REP * HEAD_DIM), lambda b, kv, i: (b, i, kv)),
        out_shape=jax.ShapeDtypeStruct((bsz, seq, Q_W), BF16),
        scratch_shapes=[pltpu.VMEM((seq, 2 * HEAD_DIM), BF16),
                        pltpu.VMEM((GQA_REP * tq, seq), BF16)],
        compiler_params=_cparams(("parallel", "parallel", "arbitrary")),
        name="attention",
    )(scores_safe, qk3, qk3, h3)


CONV_SUB = 128
CONV_HALO = 64
CONV_PAD = CONV_W // 2
CONV_SIDE_TAPS = tuple(k for k in range(CONV_W) if k != CONV_PAD)


def _conv_kernel(prev_ref, main_ref, next_ref, w_ref, b_ref, shift_ref, o_ref, win_ref, *, rt):
    r = pl.program_id(1)
    first = r == 0
    last = r == pl.num_programs(1) - 1
    prev = prev_ref[...]
    nxt = next_ref[...]
    win_ref[0:CONV_HALO, :] = jnp.where(first, jnp.zeros_like(prev), prev)
    win_ref[CONV_HALO:CONV_HALO + rt, :] = main_ref[...]
    win_ref[CONV_HALO + rt:, :] = jnp.where(last, jnp.zeros_like(nxt), nxt)
    bias = b_ref[...]
    span = CONV_SUB + 2 * CONV_HALO
    for t in range(rt // CONV_SUB):
        win = win_ref[t * CONV_SUB:t * CONV_SUB + span, :]
        centre = win[CONV_HALO:CONV_HALO + CONV_SUB].astype(F32)
        acc = bias + w_ref[CONV_PAD:CONV_PAD + 1, :] * centre
        for j, k in enumerate(CONV_SIDE_TAPS):
            acc = acc + w_ref[k:k + 1, :] * _dot(shift_ref[j], win)
        o_ref[t * CONV_SUB:(t + 1) * CONV_SUB, :] = _silu(acc).astype(BF16)


def _conv_shift_mats():
    i = np.arange(CONV_SUB)[:, None]
    j = np.arange(CONV_SUB + 2 * CONV_HALO)[None, :]
    return jnp.asarray(np.stack([j == i + CONV_HALO + (k - CONV_PAD) for k in CONV_SIDE_TAPS]), dtype=BF16)


def _conv_silu(h3, conv_w, conv_b, *, rt=1024):
    bsz, seq, _ = h3.shape
    rt = min(rt, seq)
    nh = rt // CONV_HALO
    n_halo_blk = seq // CONV_HALO
    span = CONV_SUB + 2 * CONV_HALO
    return pl.pallas_call(
        functools.partial(_conv_kernel, rt=rt),
        grid=(bsz, seq // rt, XBC_W // COL_BLK),
        in_specs=[
            pl.BlockSpec((None, CONV_HALO, COL_BLK),
                         lambda b, r, c: (b, jnp.maximum(r * nh - 1, 0), XBC_BLK0 + c)),
            pl.BlockSpec((None, rt, COL_BLK), lambda b, r, c: (b, r, XBC_BLK0 + c)),
            pl.BlockSpec((None, CONV_HALO, COL_BLK),
                         lambda b, r, c: (b, jnp.minimum((r + 1) * nh, n_halo_blk - 1), XBC_BLK0 + c)),
            pl.BlockSpec((CONV_W, COL_BLK), lambda b, r, c: (0, c)),
            pl.BlockSpec((1, COL_BLK), lambda b, r, c: (0, c)),
            pl.BlockSpec((len(CONV_SIDE_TAPS), CONV_SUB, span), lambda b, r, c: (0, 0, 0)),
        ],
        out_specs=pl.BlockSpec((None, rt, COL_BLK), lambda b, r, c: (b, r, c)),
        out_shape=jax.ShapeDtypeStruct((bsz, seq, XBC_W), BF16),
        scratch_shapes=[pltpu.VMEM((rt + 2 * CONV_HALO, COL_BLK), BF16)],
        compiler_params=_cparams(("parallel", "parallel", "arbitrary")),
        name="conv_silu",
    )(h3, h3, h3, conv_w, conv_b, _conv_shift_mats())


def _softplus(x):
    return jnp.maximum(x, 0.0) + jnp.log1p(jnp.exp(-jnp.abs(x)))


def _cumsum_pos(t01, a):
    hi, mid, lo = _split3(a)
    t2 = jnp.concatenate([t01, t01], axis=1)
    return _dot(t2, jnp.concatenate([hi, mid], axis=0)) + _dot(t01, lo)


def _ssd_kernel(*refs, backward):
    if backward:
        (xs_ref, bm_ref, cm_ref, dt_ref, bias_ref, alog_ref, ex_ref, yf_ref, z_ref, dvec_ref, ng_ref,
         o_ref, h_ref) = refs
    else:
        xs_ref, bm_ref, cm_ref, dt_ref, bias_ref, alog_ref, ex_ref, o_ref, h_ref = refs

    @pl.when(pl.program_id(1) == 0)
    def _():
        h_ref[...] = jnp.zeros_like(h_ref)

    lane = lax.broadcasted_iota(jnp.int32, (CHUNK, LANE), 1)
    row = lax.broadcasted_iota(jnp.int32, (CHUNK, LANE), 0)
    nh = N_SSD_HEADS
    hpg = HEADS_PER_GROUP

    dt = _softplus(dt_ref[...] + bias_ref[...])
    a = dt * (-LOG2E * jnp.exp(alog_ref[...]))
    tri_incl = lane <= row
    t_lo = jnp.where(tri_incl, 1.0, 0.0).astype(BF16)
    t_up = jnp.where(lane >= row, 1.0, 0.0).astype(BF16)
    cs = jnp.where(lane < nh, _cumsum_pos(t_lo, a), _cumsum_pos(t_up, a))

    off = nh if backward else 0
    if backward:
        edge = cs[0:1, :]
    else:
        edge = cs[CHUNK - 1:CHUNK, :]
    used = jnp.logical_and(lane >= off, lane < off + nh)
    e_pos = jnp.where(used, jnp.exp2(jnp.where(used, cs, 0.0)), 0.0)
    w_pos = jnp.where(used, jnp.exp2(jnp.where(used, edge - cs, 0.0)) * dt, 0.0)
    st = jnp.concatenate([e_pos, w_pos], axis=0)
    st_hi = st.astype(BF16)
    st_mid = (st - st_hi.astype(F32)).astype(BF16)
    st2 = jnp.concatenate([st_hi, st_mid], axis=1)
    if not backward:
        src_t = (cs - LOG2E * jnp.log(dt)).T

    for g in range(N_SSD_GROUPS):
        sl = slice(g * GROUP_W, (g + 1) * GROUP_W)
        sb = slice(g * D_STATE, (g + 1) * D_STATE)
        xs_f = xs_ref[:, sl].astype(F32)
        bm = bm_ref[:, sb]
        cm = cm_ref[:, sb]
        ex = _dot(st2, ex_ref[g])
        e_exp = ex[:CHUNK]
        w_exp = ex[CHUNK:]
        chunk_decay = e_exp[0:1, :] if backward else e_exp[CHUNK - 1:CHUNK, :]
        h = h_ref[g]
        y_off = _dot(cm, h.astype(BF16)) * e_exp
        s_t = _dot_tn(bm, (xs_f * w_exp).astype(BF16))
        h_ref[g] = h * chunk_decay + s_t

        if not backward:
            cb = _dot_nt(cm, bm)
            pairs = []
            for pp in range(hpg // 2):
                x_pair = xs_f[:, pp * LANE:(pp + 1) * LANE]
                ms = []
                for hh in (g * hpg + 2 * pp, g * hpg + 2 * pp + 1):
                    arg = jnp.where(tri_incl,
                                    cs[:, hh:hh + 1] - src_t[hh:hh + 1, :],
                                    cs[:, nh + hh:nh + hh + 1] - src_t[nh + hh:nh + hh + 1, :])
                    ms.append((cb * jnp.exp2(arg)).astype(BF16))
                x_a = jnp.where(lane < SSD_HEAD_DIM, x_pair, 0.0).astype(BF16)
                x_b = jnp.where(lane >= SSD_HEAD_DIM, x_pair, 0.0).astype(BF16)
                pairs.append(_dot(jnp.concatenate(ms, axis=1), jnp.concatenate([x_a, x_b], axis=0)))
            o_ref[:, sl] = jnp.concatenate(pairs, axis=1) + y_off
        else:
            y = yf_ref[:, sl] + y_off + dvec_ref[:, sl] * xs_f
            y = y * _silu(z_ref[:, sl].astype(F32))
            y = y * lax.rsqrt(jnp.mean(y * y, axis=-1, keepdims=True) + RMS_EPS) * ng_ref[:, sl]
            o_ref[:, sl] = y.astype(BF16)


def _expand_mats(off):
    k = (np.arange(2 * LANE) % LANE)[None, :, None]
    col = (np.arange(GROUP_W) // SSD_HEAD_DIM)[None, None, :]
    g = np.arange(N_SSD_GROUPS)[:, None, None]
    return jnp.asarray(k == off + HEADS_PER_GROUP * g + col, dtype=BF16)


def _ssd(xbc3, dt3, bias, alog, h3, dvec, ng):
    bsz, seq, _ = xbc3.shape
    nc = seq // CHUNK
    g_n = N_SSD_GROUPS
    scratch = [pltpu.VMEM((g_n, D_STATE, GROUP_W), F32)]
    sem = ("parallel", "arbitrary")

    def common_specs(cidx):
        return [
            pl.BlockSpec((None, CHUNK, D_INNER), lambda b, j: (b, cidx(j), 0)),
            pl.BlockSpec((None, CHUNK, BC_W), lambda b, j: (b, cidx(j), D_INNER // BC_W)),
            pl.BlockSpec((None, CHUNK, BC_W), lambda b, j: (b, cidx(j), D_INNER // BC_W + 1)),
            pl.BlockSpec((None, CHUNK, LANE), lambda b, j: (b, cidx(j), 0)),
            pl.BlockSpec((1, LANE), lambda b, j: (0, 0)),
            pl.BlockSpec((1, LANE), lambda b, j: (0, 0)),
            pl.BlockSpec((g_n, 2 * LANE, GROUP_W), lambda b, j: (0, 0, 0)),
        ]

    fwd = lambda j: j
    y_f = pl.pallas_call(
        functools.partial(_ssd_kernel, backward=False),
        grid=(bsz, nc),
        in_specs=common_specs(fwd),
        out_specs=pl.BlockSpec((None, CHUNK, D_INNER), lambda b, j: (b, j, 0)),
        out_shape=jax.ShapeDtypeStruct((bsz, seq, D_INNER), F32),
        scratch_shapes=scratch,
        compiler_params=_cparams(sem),
        name="ssd_fwd",
    )(xbc3, xbc3, xbc3, dt3, bias, alog, _expand_mats(0))

    bwd = lambda j: nc - 1 - j
    return pl.pallas_call(
        functools.partial(_ssd_kernel, backward=True),
        grid=(bsz, nc),
        in_specs=common_specs(bwd) + [
            pl.BlockSpec((None, CHUNK, D_INNER), lambda b, j: (b, bwd(j), 0)),
            pl.BlockSpec((None, CHUNK, D_INNER), lambda b, j: (b, bwd(j), Z_BLK0 * COL_BLK // D_INNER)),
            pl.BlockSpec((1, D_INNER), lambda b, j: (0, 0)),
            pl.BlockSpec((1, D_INNER), lambda b, j: (0, 0)),
        ],
        out_specs=pl.BlockSpec((None, CHUNK, D_INNER), lambda b, j: (b, bwd(j), 0)),
        out_shape=jax.ShapeDtypeStruct((bsz, seq, D_INNER), BF16),
        scratch_shapes=scratch,
        compiler_params=_cparams(sem),
        name="ssd_bwd",
    )(xbc3, xbc3, xbc3, dt3, bias, alog, _expand_mats(N_SSD_HEADS), y_f, h3, dvec, ng)


def _branch_mix_kernel(ya_ref, ys_ref, ga_ref, gs_ref, wa_ref, ws_ref, o_ref):
    pa = jax.nn.sigmoid(ga_ref[...].astype(F32)) * _dot(ya_ref[...], wa_ref[...])
    ps = jax.nn.sigmoid(gs_ref[...].astype(F32)) * _dot(ys_ref[...], ws_ref[...])
    o_ref[...] = (pa + ps).astype(BF16)


def _branch_mix(y_attn, y_ssd, gate, wa, ws, *, tm=1024, tn=512):
    m = y_attn.shape[0]
    d = wa.shape[1]
    tm = min(tm, m)
    n_blk = d // tn
    return pl.pallas_call(
        _branch_mix_kernel,
        grid=(m // tm, n_blk),
        in_specs=[
            pl.BlockSpec((tm, Q_W), lambda i, n: (i, 0)),
            pl.BlockSpec((tm, D_INNER), lambda i, n: (i, 0)),
            pl.BlockSpec((tm, tn), lambda i, n: (i, n)),
            pl.BlockSpec((tm, tn), lambda i, n: (i, n_blk + n)),
            pl.BlockSpec((Q_W, tn), lambda i, n: (0, n)),
            pl.BlockSpec((D_INNER, tn), lambda i, n: (0, n)),
        ],
        out_specs=pl.BlockSpec((tm, tn), lambda i, n: (i, n)),
        out_shape=jax.ShapeDtypeStruct((m, d), BF16),
        compiler_params=_cparams(("parallel", "arbitrary")),
        name="branch_mix",
    )(y_attn, y_ssd, gate, gate, wa, ws)


def _proj_ln_kernel(a_ref, x_ref, w_ref, g_ref, b_ref, o_ref, *maybe_ob_ref, scale):
    y = _dot(a_ref[...], w_ref[...])
    if scale != 1.0:
        y = scale * y
    o = _layer_norm(ALPHA * x_ref[...] + y, g_ref[...], b_ref[...])
    o_ref[...] = o
    for ob_ref in maybe_ob_ref:
        ob_ref[...] = o.astype(BF16)


def _proj_ln(a, x, w, g, b, *, scale=1.0, tm=512, emit_bf16=True):
    m, d = x.shape
    k = a.shape[1]
    tm = min(tm, m)
    n_out = 2 if emit_bf16 else 1
    return pl.pallas_call(
        functools.partial(_proj_ln_kernel, scale=scale),
        grid=(m // tm,),
        in_specs=[
            pl.BlockSpec((tm, k), lambda i: (i, 0)),
            pl.BlockSpec((tm, d), lambda i: (i, 0)),
            pl.BlockSpec((k, d), lambda i: (0, 0), pipeline_mode=pl.Buffered(1)),
            pl.BlockSpec((1, d), lambda i: (0, 0)),
            pl.BlockSpec((1, d), lambda i: (0, 0)),
        ],
        out_specs=[pl.BlockSpec((tm, d), lambda i: (i, 0))] * n_out,
        out_shape=[jax.ShapeDtypeStruct((m, d), F32), jax.ShapeDtypeStruct((m, d), BF16)][:n_out],
        compiler_params=_cparams(("parallel",)),
        name="proj_ln",
    )(a, x, w, g, b)


def _cross_kernel(x_ref, xb_ref, wq_ref, kv_ref, wo_ref, g_ref, b_ref, o_ref):
    scale = MEM_HEAD_DIM ** -0.5
    q = (_dot(xb_ref[...], wq_ref[...]) * scale).astype(BF16)
    outs = []
    for hh in range(N_MEM_HEADS):
        sl = slice(hh * MEM_HEAD_DIM, (hh + 1) * MEM_HEAD_DIM)
        k = kv_ref[:, sl]
        v = kv_ref[:, MEM_W + hh * MEM_HEAD_DIM:MEM_W + (hh + 1) * MEM_HEAD_DIM]
        s = _dot_nt(q[:, sl], k)
        p = jnp.exp(s - jnp.max(s, axis=-1, keepdims=True))
        p = p / jnp.sum(p, axis=-1, keepdims=True)
        outs.append(_dot(p.astype(BF16), v))
    o_att = jnp.concatenate(outs, axis=1).astype(BF16)
    y = ALPHA * x_ref[...] + _dot(o_att, wo_ref[...])
    o_ref[...] = _layer_norm(y, g_ref[...], b_ref[...])


def _cross(x2, x2b, wq, kv3, wo, g, b, *, seq, tm=512):
    m, d = x2.shape
    tm = min(tm, seq)
    t_per_seq = seq // tm
    mem_len = kv3.shape[1]
    return pl.pallas_call(
        _cross_kernel,
        grid=(m // tm,),
        in_specs=[
            pl.BlockSpec((tm, d), lambda i: (i, 0)),
            pl.BlockSpec((tm, d), lambda i: (i, 0)),
            pl.BlockSpec((d, MEM_W), lambda i: (0, 0)),
            pl.BlockSpec((None, mem_len, 2 * MEM_W), lambda i: (i // t_per_seq, 0, 0)),
            pl.BlockSpec((MEM_W, d), lambda i: (0, 0)),
            pl.BlockSpec((1, d), lambda i: (0, 0)),
            pl.BlockSpec((1, d), lambda i: (0, 0)),
        ],
        out_specs=pl.BlockSpec((tm, d), lambda i: (i, 0)),
        out_shape=jax.ShapeDtypeStruct((m, d), F32),
        compiler_params=_cparams(("parallel",)),
        name="cross_attn",
    )(x2, x2b, wq, kv3, wo, g, b)


def _rope_tables(seq):
    t = np.arange(seq)
    inv_freq = ROPE_THETA ** (-np.arange(ROPE_AXIS_DIM // 2, dtype=np.float64) * 2.0 / ROPE_AXIS_DIM)
    ang_r = (t // GRID_W)[:, None] * inv_freq[None, :]
    ang_c = (t % GRID_W)[:, None] * inv_freq[None, :]
    cr, sr, cc, sc = np.cos(ang_r), np.sin(ang_r), np.cos(ang_c), np.sin(ang_c)
    rc = np.concatenate([cr, cr, cc, cc], axis=1)
    rs = np.concatenate([-sr, sr, -sc, sc], axis=1)
    lane = np.arange(HEAD_DIM)
    swap = (lane[:, None] == (lane[None, :] ^ (ROPE_AXIS_DIM // 2)))
    swap2 = np.concatenate([swap, swap], axis=0)
    return (jnp.asarray(rc.astype(np.float32)), jnp.asarray(rs.astype(np.float32)),
            jnp.asarray(swap2, dtype=BF16))


def kernel(x, mem, ffn1_w1, ffn1_w3, ffn1_w2, ln1_g, ln1_b, w_in, q_norm_g, k_norm_g, conv_w, conv_b,
           dt_bias, a_log, ssd_d, ssd_norm_g, w_branch_attn, w_branch_ssd, w_mix_out, ln2_g, ln2_b,
           mem_wq, mem_wk, mem_wv, mem_wo, ln3_g, ln3_b, ffn2_w1, ffn2_w3, ffn2_w2, ln4_g, ln4_b):
    bsz, seq, d = x.shape
    m = bsz * seq
    mem_len = mem.shape[1]
    bf = lambda t: t.astype(BF16)
    rc, rs, swap = _rope_tables(seq)
    xc = x.reshape(m, d)
    for l in range(DEPTH):
        row = lambda v: v[l].reshape(1, -1)
        act = _glu(xc, bf(ffn1_w1[l]), bf(ffn1_w3[l]))
        x1, x1b = _proj_ln(act, xc, bf(ffn1_w2[l]), row(ln1_g), row(ln1_b), scale=0.5, tm=256)
        dt0 = Q_W + 2 * KV_W + D_INNER + XBC_W
        w_all = bf(w_in[l])
        gains = jnp.concatenate([jnp.tile(q_norm_g[l] * (HEAD_DIM ** -0.5 * LOG2E), N_Q_HEADS),
                                 jnp.tile(k_norm_g[l], N_KV_HEADS)]).reshape(1, -1)
        qk = _qk_proj(x1b, w_all, gains, rc, rs, swap, seq=seq)
        h = _matmul(x1b, w_all, BF16, tm=2048, tn=COL_BLK, n_out=H_W, col0=SRC_ROPE_BLKS * COL_BLK,
                    out_rot=1)
        dt_raw = _matmul(x1b, w_all, F32, tm=1024, tn=DT_W, n_out=DT_W, col0=dt0)
        gate = _matmul(x1b, w_all, BF16, tm=2048, tn=COL_BLK, n_out=2 * D_MODEL, col0=dt0 + DT_W,
                       strips=COL_BLK // LANE)
        h3 = h.reshape(bsz, seq, H_W)
        score_bound = HEAD_DIM * jnp.max(jnp.abs(gains[0, :Q_W])) * jnp.max(jnp.abs(gains[0, Q_W:]))
        y_attn = _attention(qk.reshape(bsz, seq, SRC_ROPE_BLKS * COL_BLK), h3,
                            (score_bound < SCORE_SAFE_LOG2).astype(jnp.int32).reshape(1))
        xbc = _conv_silu(h3, conv_w[l], conv_b[l].reshape(1, -1))
        y_ssd = _ssd(xbc, dt_raw.reshape(bsz, seq, DT_W), dt_bias[l].reshape(1, DT_W),
                     a_log[l].reshape(1, DT_W), h3,
                     jnp.repeat(ssd_d[l], SSD_HEAD_DIM).reshape(1, D_INNER),
                     ssd_norm_g[l].reshape(1, D_INNER))
        merged = _branch_mix(y_attn.reshape(m, Q_W), y_ssd.reshape(m, D_INNER), gate,
                             bf(w_branch_attn[l]), bf(w_branch_ssd[l]))
        x2, x2b = _proj_ln(merged, x1, bf(w_mix_out[l]), row(ln2_g), row(ln2_b))
        w_kv = bf(jnp.concatenate([mem_wk[l], mem_wv[l]], axis=1))
        kv = _matmul(bf(mem.reshape(bsz * mem_len, d)), w_kv, BF16, tm=bsz * mem_len, tn=COL_BLK)
        x3 = _cross(x2, x2b, bf(mem_wq[l]), kv.reshape(bsz, mem_len, 2 * MEM_W), bf(mem_wo[l]),
                       row(ln3_g), row(ln3_b), seq=seq)
        act = _glu(x3, bf(ffn2_w1[l]), bf(ffn2_w3[l]))
        (xc,) = _proj_ln(act, x3, bf(ffn2_w2[l]), row(ln4_g), row(ln4_b), scale=0.5, tm=256,
                         emit_bf16=False)
    return xc.reshape(bsz, seq, d)
```

```python
import functools
import math

import jax
import jax.numpy as jnp
import numpy as np
from jax import lax
from jax.experimental import pallas as pl
from jax.experimental.pallas import tpu as pltpu

F32 = jnp.float32
BF16 = jnp.bfloat16

D_MODEL = 2048
DEPTH = 1
GRID_W = 64
HEAD_DIM = 128
N_Q_HEADS = 16
N_KV_HEADS = 4
GQA_REP = 4
ROPE_AXIS_DIM = 64
ROPE_THETA = 10000.0
D_INNER = 4096
SSD_HEAD_DIM = 64
N_SSD_HEADS = 64
N_SSD_GROUPS = 8
HEADS_PER_GROUP = N_SSD_HEADS // N_SSD_GROUPS
GROUP_W = D_INNER // N_SSD_GROUPS
D_STATE = 128
CONV_W = 5
CHUNK = 128
N_MEM_HEADS = 4
MEM_HEAD_DIM = 128
D_FF = 5632
LN_EPS = 1e-5
RMS_EPS = 1e-6
Q_W = 2048
KV_W = 512
BC_W = 1024
XBC_W = 6144
DT_W = 128
MEM_W = 512
ALPHA = (2.0 * DEPTH) ** 0.25
LOG2E = math.log2(math.e)

LANE = 128
COL_BLK = 512
SRC_ROPE_BLKS = (Q_W + KV_W) // COL_BLK
Z_BLK0 = 0
XBC_BLK0 = Z_BLK0 + D_INNER // COL_BLK
V_BLK = XBC_BLK0 + XBC_W // COL_BLK
H_BLKS = V_BLK + 1
H_W = H_BLKS * COL_BLK

VMEM_LIMIT = 56 * 1024 * 1024


def _cparams(sem):
    return pltpu.CompilerParams(dimension_semantics=sem, vmem_limit_bytes=VMEM_LIMIT)


def _dot(a, b):
    return jnp.dot(a, b, preferred_element_type=F32)


def _dot_nt(a, b):
    return lax.dot_general(a, b, (((1,), (1,)), ((), ())), preferred_element_type=F32)


def _dot_tn(a, b):
    return lax.dot_general(a, b, (((0,), (0,)), ((), ())), preferred_element_type=F32)


def _split3(v):
    hi = v.astype(BF16)
    r1 = v - hi.astype(F32)
    mid = r1.astype(BF16)
    lo = (r1 - mid.astype(F32)).astype(BF16)
    return hi, mid, lo


def _layer_norm(y, g, b):
    mu = jnp.mean(y, axis=-1, keepdims=True)
    d = y - mu
    var = jnp.mean(d * d, axis=-1, keepdims=True)
    return d * lax.rsqrt(var + LN_EPS) * g + b


def _silu(x):
    return x * jax.nn.sigmoid(x)


def _glu_kernel(x_ref, w1_ref, w3_ref, *refs, cast_blocks):
    n_cast = len(cast_blocks)
    cast_in = refs[:n_cast]
    o_ref = refs[n_cast]
    cast_out = refs[n_cast + 1:2 * n_cast + 1]
    xb_ref = refs[-1]

    @pl.when(pl.program_id(1) == 0)
    def _():
        xb_ref[...] = x_ref[...].astype(BF16)

    xb = xb_ref[...]
    o_ref[...] = (_silu(_dot(xb, w1_ref[...])) * _dot(xb, w3_ref[...])).astype(BF16)

    step = pl.program_id(0) * pl.num_programs(1) + pl.program_id(1)
    for src, dst, n_blk in zip(cast_in, cast_out, cast_blocks):
        @pl.when(step < n_blk)
        def _(src=src, dst=dst):
            dst[...] = src[...].astype(BF16)


def _cast_row_block(rows, n_steps):
    bf16_sublanes = 16
    for rb in range(bf16_sublanes, rows + 1, bf16_sublanes):
        if rows % rb == 0 and rows // rb <= n_steps:
            return rb
    raise ValueError((rows, n_steps))


def _glu(x, w1, w3, cast=(), *, tm=1024, tf=512):
    m, d = x.shape
    dff = w1.shape[1]
    tm = min(tm, m)
    n_f = dff // tf
    n_steps = (m // tm) * n_f
    def cast_specs():
        specs = []
        for w in cast:
            rb = _cast_row_block(w.shape[0], n_steps)
            n_blk = w.shape[0] // rb
            specs.append(pl.BlockSpec((rb, w.shape[1]),
                                      lambda i, f, n_blk=n_blk: (jnp.minimum(i * n_f + f, n_blk - 1), 0)))
        return specs

    cast_blocks = [w.shape[0] // _cast_row_block(w.shape[0], n_steps) for w in cast]
    return pl.pallas_call(
        functools.partial(_glu_kernel, cast_blocks=tuple(cast_blocks)),
        grid=(m // tm, n_f),
        in_specs=[
            pl.BlockSpec((tm, d), lambda i, f: (i, 0)),
            pl.BlockSpec((d, tf), lambda i, f: (0, f)),
            pl.BlockSpec((d, tf), lambda i, f: (0, f)),
        ] + cast_specs(),
        out_specs=[pl.BlockSpec((tm, tf), lambda i, f: (i, f))] + cast_specs(),
        out_shape=[jax.ShapeDtypeStruct((m, dff), BF16)]
                  + [jax.ShapeDtypeStruct(w.shape, BF16) for w in cast],
        scratch_shapes=[pltpu.VMEM((tm, d), BF16)],
        compiler_params=_cparams(("arbitrary", "arbitrary")),
        name="glu",
    )(x, w1, w3, *cast)


def _qk_proj_kernel(x_ref, w_ref, gain_ref, rc_ref, rs_ref, swap_ref, o_ref):
    acc = _dot(x_ref[...], w_ref[...])
    rc = rc_ref[...]
    rs = rs_ref[...]
    swap = swap_ref[...]
    for hh in range(COL_BLK // HEAD_DIM):
        sl = slice(hh * HEAD_DIM, (hh + 1) * HEAD_DIM)
        xh = acc[:, sl]
        ms = jnp.mean(xh * xh, axis=-1, keepdims=True)
        xh = xh * lax.rsqrt(ms + RMS_EPS) * gain_ref[:, sl]
        hi = xh.astype(BF16)
        mid = (xh - hi.astype(F32)).astype(BF16)
        partner = _dot(jnp.concatenate([hi, mid], axis=1), swap)
        o_ref[:, sl] = (xh * rc + partner * rs).astype(BF16)


def _qk_proj(xb, w, gains, rc, rs, swap, *, seq, tm=2048):
    m, d = xb.shape
    tm = min(tm, seq)
    t_per_seq = seq // tm
    return pl.pallas_call(
        _qk_proj_kernel,
        grid=(m // tm, SRC_ROPE_BLKS),
        in_specs=[
            pl.BlockSpec((tm, d), lambda i, n: (i, 0)),
            pl.BlockSpec((d, COL_BLK), lambda i, n: (0, n)),
            pl.BlockSpec((1, COL_BLK), lambda i, n: (0, n)),
            pl.BlockSpec((tm, HEAD_DIM), lambda i, n: (i % t_per_seq, 0)),
            pl.BlockSpec((tm, HEAD_DIM), lambda i, n: (i % t_per_seq, 0)),
            pl.BlockSpec((2 * HEAD_DIM, HEAD_DIM), lambda i, n: (0, 0)),
        ],
        out_specs=pl.BlockSpec((tm, COL_BLK), lambda i, n: (i, n)),
        out_shape=jax.ShapeDtypeStruct((m, SRC_ROPE_BLKS * COL_BLK), BF16),
        compiler_params=_cparams(("parallel", "arbitrary")),
        name="qk_proj",
    )(xb, w, gains, rc, rs, swap)


def _matmul_kernel(x_ref, *refs):
    *w_refs, o_ref = refs
    w = w_refs[0][...] if len(w_refs) == 1 else jnp.concatenate([r[...] for r in w_refs], axis=1)
    o_ref[...] = _dot(x_ref[...], w).astype(o_ref.dtype)


def _matmul(xb, w, out_dtype, *, tm, tn, n_out=None, col0=0, out_rot=0, strips=1):
    m, d = xb.shape
    n = w.shape[1] if n_out is None else n_out
    tm = min(tm, m)
    n_blk = n // tn
    ts = tn // strips
    s0 = col0 // ts
    w_spec = lambda s: pl.BlockSpec((d, ts), lambda i, j: (0, s0 + strips * j + s))
    return pl.pallas_call(
        _matmul_kernel,
        grid=(m // tm, n_blk),
        in_specs=[pl.BlockSpec((tm, d), lambda i, j: (i, 0))] + [w_spec(s) for s in range(strips)],
        out_specs=pl.BlockSpec((tm, tn), lambda i, j: (i, (j + n_blk - out_rot) % n_blk)),
        out_shape=jax.ShapeDtypeStruct((m, n), out_dtype),
        compiler_params=_cparams(("parallel", "arbitrary")),
        name="matmul",
    )(xb, *([w] * strips))


SCORE_SAFE_LOG2 = 60.0


def _attn_kernel(safe_ref, q_ref, k_ref, v_ref, o_ref, v1_ref, p_ref, *, tq, tk):
    seq = k_ref.shape[0]

    @pl.when(pl.program_id(2) == 0)
    def _():
        v1_ref[:, :HEAD_DIM] = v_ref[...]
        v1_ref[:, HEAD_DIM:] = jnp.ones((seq, HEAD_DIM), BF16)

    q = jnp.concatenate([q_ref[:, hh * HEAD_DIM:(hh + 1) * HEAD_DIM] for hh in range(GQA_REP)], axis=0)
    rows = GQA_REP * tq

    def finish(acc):
        o = (acc[:, :HEAD_DIM] / acc[:, HEAD_DIM:HEAD_DIM + 1]).astype(BF16)
        for hh in range(GQA_REP):
            o_ref[:, hh * HEAD_DIM:(hh + 1) * HEAD_DIM] = o[hh * tq:(hh + 1) * tq]

    @pl.when(safe_ref[0] == 1)
    def _():
        for j in range(seq // tk):
            s = _dot_nt(q, k_ref[j * tk:(j + 1) * tk, :])
            p_ref[:, j * tk:(j + 1) * tk] = jnp.exp2(s).astype(BF16)
        finish(_dot(p_ref[...], v1_ref[...]))

    @pl.when(safe_ref[0] == 0)
    def _():
        m_i = jnp.full((rows, 1), -jnp.inf, F32)
        acc = jnp.zeros((rows, 2 * HEAD_DIM), F32)
        for j in range(seq // tk):
            s = _dot_nt(q, k_ref[j * tk:(j + 1) * tk, :])
            m_new = jnp.maximum(m_i, jnp.max(s, axis=-1, keepdims=True))
            p = jnp.exp2(s - m_new).astype(BF16)
            acc = jnp.exp2(m_i - m_new) * acc + _dot(p, v1_ref[j * tk:(j + 1) * tk, :])
            m_i = m_new
        finish(acc)


def _attention(qk3, h3, scores_safe, *, tq=512, tk=512):
    bsz, seq, _ = h3.shape
    tq = min(tq, seq)
    tk = min(tk, seq)
    k_blk0 = Q_W // HEAD_DIM
    v_blk0 = V_BLK * COL_BLK // HEAD_DIM
    return pl.pallas_call(
        functools.partial(_attn_kernel, tq=tq, tk=tk),
        grid=(bsz, N_KV_HEADS, seq // tq),
        in_specs=[
            pl.BlockSpec(memory_space=pltpu.SMEM),
            pl.BlockSpec((None, tq, GQA_REP * HEAD_DIM), lambda b, kv, i: (b, i, kv)),
            pl.BlockSpec((None, seq, HEAD_DIM), lambda b, kv, i: (b, 0, k_blk0 + kv)),
            pl.BlockSpec((None, seq, HEAD_DIM), lambda b, kv, i: (b, 0, v_blk0 + kv)),
        ],
        out_specs=pl.BlockSpec((None, tq, GQA_REP * HEAD_DIM), lambda b, kv, i: (b, i, kv)),
        out_shape=jax.ShapeDtypeStruct((bsz, seq, Q_W), BF16),
        scratch_shapes=[pltpu.VMEM((seq, 2 * HEAD_DIM), BF16),
                        pltpu.VMEM((GQA_REP * tq, seq), BF16)],
        compiler_params=_cparams(("parallel", "parallel", "arbitrary")),
        name="attention",
    )(scores_safe, qk3, qk3, h3)


CONV_SUB = 128
CONV_HALO = 64
CONV_PAD = CONV_W // 2
CONV_SIDE_TAPS = tuple(k for k in range(CONV_W) if k != CONV_PAD)


def _conv_kernel(prev_ref, main_ref, next_ref, w_ref, b_ref, shift_ref, o_ref, win_ref, *, rt):
    r = pl.program_id(1)
    first = r == 0
    last = r == pl.num_programs(1) - 1
    prev = prev_ref[...]
    nxt = next_ref[...]
    win_ref[0:CONV_HALO, :] = jnp.where(first, jnp.zeros_like(prev), prev)
    win_ref[CONV_HALO:CONV_HALO + rt, :] = main_ref[...]
    win_ref[CONV_HALO + rt:, :] = jnp.where(last, jnp.zeros_like(nxt), nxt)
    bias = b_ref[...]
    span = CONV_SUB + 2 * CONV_HALO
    for t in range(rt // CONV_SUB):
        win = win_ref[t * CONV_SUB:t * CONV_SUB + span, :]
        centre = win[CONV_HALO:CONV_HALO + CONV_SUB].astype(F32)
        acc = bias + w_ref[CONV_PAD:CONV_PAD + 1, :] * centre
        for j, k in enumerate(CONV_SIDE_TAPS):
            acc = acc + w_ref[k:k + 1, :] * _dot(shift_ref[j], win)
        o_ref[t * CONV_SUB:(t + 1) * CONV_SUB, :] = _silu(acc).astype(BF16)


def _conv_shift_mats():
    i = np.arange(CONV_SUB)[:, None]
    j = np.arange(CONV_SUB + 2 * CONV_HALO)[None, :]
    return jnp.asarray(np.stack([j == i + CONV_HALO + (k - CONV_PAD) for k in CONV_SIDE_TAPS]), dtype=BF16)


def _conv_silu(h3, conv_w, conv_b, *, rt=1024, cw=1024):
    bsz, seq, _ = h3.shape
    rt = min(rt, seq)
    nh = rt // CONV_HALO
    n_halo_blk = seq // CONV_HALO
    span = CONV_SUB + 2 * CONV_HALO
    c0 = XBC_BLK0 * COL_BLK // cw
    return pl.pallas_call(
        functools.partial(_conv_kernel, rt=rt),
        grid=(bsz, seq // rt, XBC_W // cw),
        in_specs=[
            pl.BlockSpec((None, CONV_HALO, cw), lambda b, r, c: (b, jnp.maximum(r * nh - 1, 0), c0 + c)),
            pl.BlockSpec((None, rt, cw), lambda b, r, c: (b, r, c0 + c)),
            pl.BlockSpec((None, CONV_HALO, cw),
                         lambda b, r, c: (b, jnp.minimum((r + 1) * nh, n_halo_blk - 1), c0 + c)),
            pl.BlockSpec((CONV_W, cw), lambda b, r, c: (0, c)),
            pl.BlockSpec((1, cw), lambda b, r, c: (0, c)),
            pl.BlockSpec((len(CONV_SIDE_TAPS), CONV_SUB, span), lambda b, r, c: (0, 0, 0)),
        ],
        out_specs=pl.BlockSpec((None, rt, cw), lambda b, r, c: (b, r, c)),
        out_shape=jax.ShapeDtypeStruct((bsz, seq, XBC_W), BF16),
        scratch_shapes=[pltpu.VMEM((rt + 2 * CONV_HALO, cw), BF16)],
        compiler_params=_cparams(("parallel", "parallel", "arbitrary")),
        name="conv_silu",
    )(h3, h3, h3, conv_w, conv_b, _conv_shift_mats())


def _softplus(x):
    return jnp.maximum(x, 0.0) + jnp.log1p(jnp.exp(-jnp.abs(x)))


def _cumsum_pos(t01, a):
    hi, mid, lo = _split3(a)
    t2 = jnp.concatenate([t01, t01], axis=1)
    return _dot(t2, jnp.concatenate([hi, mid], axis=0)) + _dot(t01, lo)


def _ssd_kernel(*refs, backward):
    if backward:
        (xs_ref, bm_ref, cm_ref, dt_ref, bias_ref, alog_ref, ex_ref, yf_ref, z_ref, dvec_ref, ng_ref,
         o_ref, h_ref) = refs
    else:
        xs_ref, bm_ref, cm_ref, dt_ref, bias_ref, alog_ref, ex_ref, o_ref, h_ref = refs

    @pl.when(pl.program_id(1) == 0)
    def _():
        h_ref[...] = jnp.zeros_like(h_ref)

    lane = lax.broadcasted_iota(jnp.int32, (CHUNK, LANE), 1)
    row = lax.broadcasted_iota(jnp.int32, (CHUNK, LANE), 0)
    nh = N_SSD_HEADS
    hpg = HEADS_PER_GROUP

    dt = _softplus(dt_ref[...] + bias_ref[...])
    a = dt * (-LOG2E * jnp.exp(alog_ref[...]))
    tri_incl = lane <= row
    t_lo = jnp.where(tri_incl, 1.0, 0.0).astype(BF16)
    t_up = jnp.where(lane >= row, 1.0, 0.0).astype(BF16)
    cs = jnp.where(lane < nh, _cumsum_pos(t_lo, a), _cumsum_pos(t_up, a))

    off = nh if backward else 0
    if backward:
        edge = cs[0:1, :]
    else:
        edge = cs[CHUNK - 1:CHUNK, :]
    used = jnp.logical_and(lane >= off, lane < off + nh)
    e_pos = jnp.where(used, jnp.exp2(jnp.where(used, cs, 0.0)), 0.0)
    w_pos = jnp.where(used, jnp.exp2(jnp.where(used, edge - cs, 0.0)) * dt, 0.0)
    st = jnp.concatenate([e_pos, w_pos], axis=0)
    st_hi = st.astype(BF16)
    st_mid = (st - st_hi.astype(F32)).astype(BF16)
    st2 = jnp.concatenate([st_hi, st_mid], axis=1)
    if not backward:
        src_t = (cs - LOG2E * jnp.log(dt)).T

    for g in range(N_SSD_GROUPS):
        sl = slice(g * GROUP_W, (g + 1) * GROUP_W)
        sb = slice(g * D_STATE, (g + 1) * D_STATE)
        xs_f = xs_ref[:, sl].astype(F32)
        bm = bm_ref[:, sb]
        cm = cm_ref[:, sb]
        ex = _dot(st2, ex_ref[g])
        e_exp = ex[:CHUNK]
        w_exp = ex[CHUNK:]
        chunk_decay = e_exp[0:1, :] if backward else e_exp[CHUNK - 1:CHUNK, :]
        h = h_ref[g]
        y_off = _dot(cm, h.astype(BF16)) * e_exp
        s_t = _dot_tn(bm, (xs_f * w_exp).astype(BF16))
        h_ref[g] = h * chunk_decay + s_t

        if not backward:
            cb = _dot_nt(cm, bm)
            pairs = []
            for pp in range(hpg // 2):
                x_pair = xs_f[:, pp * LANE:(pp + 1) * LANE]
                ms = []
                for hh in (g * hpg + 2 * pp, g * hpg + 2 * pp + 1):
                    arg = jnp.where(tri_incl,
                                    cs[:, hh:hh + 1] - src_t[hh:hh + 1, :],
                                    cs[:, nh + hh:nh + hh + 1] - src_t[nh + hh:nh + hh + 1, :])
                    ms.append((cb * jnp.exp2(arg)).astype(BF16))
                x_a = jnp.where(lane < SSD_HEAD_DIM, x_pair, 0.0).astype(BF16)
                x_b = jnp.where(lane >= SSD_HEAD_DIM, x_pair, 0.0).astype(BF16)
                pairs.append(_dot(jnp.concatenate(ms, axis=1), jnp.concatenate([x_a, x_b], axis=0)))
            o_ref[:, sl] = jnp.concatenate(pairs, axis=1) + y_off
        else:
            y = yf_ref[:, sl] + y_off + dvec_ref[:, sl] * xs_f
            y = y * _silu(z_ref[:, sl].astype(F32))
            y = y * lax.rsqrt(jnp.mean(y * y, axis=-1, keepdims=True) + RMS_EPS) * ng_ref[:, sl]
            o_ref[:, sl] = y.astype(BF16)


def _expand_mats(off):
    k = (np.arange(2 * LANE) % LANE)[None, :, None]
    col = (np.arange(GROUP_W) // SSD_HEAD_DIM)[None, None, :]
    g = np.arange(N_SSD_GROUPS)[:, None, None]
    return jnp.asarray(k == off + HEADS_PER_GROUP * g + col, dtype=BF16)


def _ssd(xbc3, dt3, bias, alog, h3, dvec, ng):
    bsz, seq, _ = xbc3.shape
    nc = seq // CHUNK
    g_n = N_SSD_GROUPS
    scratch = [pltpu.VMEM((g_n, D_STATE, GROUP_W), F32)]
    sem = ("parallel", "arbitrary")

    def common_specs(cidx):
        return [
            pl.BlockSpec((None, CHUNK, D_INNER), lambda b, j: (b, cidx(j), 0)),
            pl.BlockSpec((None, CHUNK, BC_W), lambda b, j: (b, cidx(j), D_INNER // BC_W)),
            pl.BlockSpec((None, CHUNK, BC_W), lambda b, j: (b, cidx(j), D_INNER // BC_W + 1)),
            pl.BlockSpec((None, CHUNK, LANE), lambda b, j: (b, cidx(j), 0)),
            pl.BlockSpec((1, LANE), lambda b, j: (0, 0)),
            pl.BlockSpec((1, LANE), lambda b, j: (0, 0)),
            pl.BlockSpec((g_n, 2 * LANE, GROUP_W), lambda b, j: (0, 0, 0)),
        ]

    fwd = lambda j: j
    y_f = pl.pallas_call(
        functools.partial(_ssd_kernel, backward=False),
        grid=(bsz, nc),
        in_specs=common_specs(fwd),
        out_specs=pl.BlockSpec((None, CHUNK, D_INNER), lambda b, j: (b, j, 0)),
        out_shape=jax.ShapeDtypeStruct((bsz, seq, D_INNER), F32),
        scratch_shapes=scratch,
        compiler_params=_cparams(sem),
        name="ssd_fwd",
    )(xbc3, xbc3, xbc3, dt3, bias, alog, _expand_mats(0))

    bwd = lambda j: nc - 1 - j
    return pl.pallas_call(
        functools.partial(_ssd_kernel, backward=True),
        grid=(bsz, nc),
        in_specs=common_specs(bwd) + [
            pl.BlockSpec((None, CHUNK, D_INNER), lambda b, j: (b, bwd(j), 0)),
            pl.BlockSpec((None, CHUNK, D_INNER), lambda b, j: (b, bwd(j), Z_BLK0 * COL_BLK // D_INNER)),
            pl.BlockSpec((1, D_INNER), lambda b, j: (0, 0)),
            pl.BlockSpec((1, D_INNER), lambda b, j: (0, 0)),
        ],
        out_specs=pl.BlockSpec((None, CHUNK, D_INNER), lambda b, j: (b, bwd(j), 0)),
        out_shape=jax.ShapeDtypeStruct((bsz, seq, D_INNER), BF16),
        scratch_shapes=scratch,
        compiler_params=_cparams(sem),
        name="ssd_bwd",
    )(xbc3, xbc3, xbc3, dt3, bias, alog, _expand_mats(N_SSD_HEADS), y_f, h3, dvec, ng)


def _branch_mix_kernel(ya_ref, ys_ref, ga_ref, gs_ref, wa_ref, ws_ref, o_ref):
    pa = jax.nn.sigmoid(ga_ref[...].astype(F32)) * _dot(ya_ref[...], wa_ref[...])
    ps = jax.nn.sigmoid(gs_ref[...].astype(F32)) * _dot(ys_ref[...], ws_ref[...])
    o_ref[...] = (pa + ps).astype(BF16)


def _branch_mix(y_attn, y_ssd, gate, wa, ws, *, tm=1024, tn=512):
    m = y_attn.shape[0]
    d = wa.shape[1]
    tm = min(tm, m)
    n_blk = d // tn
    return pl.pallas_call(
        _branch_mix_kernel,
        grid=(m // tm, n_blk),
        in_specs=[
            pl.BlockSpec((tm, Q_W), lambda i, n: (i, 0)),
            pl.BlockSpec((tm, D_INNER), lambda i, n: (i, 0)),
            pl.BlockSpec((tm, tn), lambda i, n: (i, n)),
            pl.BlockSpec((tm, tn), lambda i, n: (i, n_blk + n)),
            pl.BlockSpec((Q_W, tn), lambda i, n: (0, n)),
            pl.BlockSpec((D_INNER, tn), lambda i, n: (0, n)),
        ],
        out_specs=pl.BlockSpec((tm, tn), lambda i, n: (i, n)),
        out_shape=jax.ShapeDtypeStruct((m, d), BF16),
        compiler_params=_cparams(("parallel", "arbitrary")),
        name="branch_mix",
    )(y_attn, y_ssd, gate, gate, wa, ws)


def _proj_ln_kernel(a_ref, x_ref, w_ref, g_ref, b_ref, o_ref, *maybe_ob_ref, scale):
    y = _dot(a_ref[...], w_ref[...])
    if scale != 1.0:
        y = scale * y
    o = _layer_norm(ALPHA * x_ref[...] + y, g_ref[...], b_ref[...])
    o_ref[...] = o
    for ob_ref in maybe_ob_ref:
        ob_ref[...] = o.astype(BF16)


def _proj_ln(a, x, w, g, b, *, scale=1.0, tm=512, emit_bf16=True):
    m, d = x.shape
    k = a.shape[1]
    tm = min(tm, m)
    n_out = 2 if emit_bf16 else 1
    return pl.pallas_call(
        functools.partial(_proj_ln_kernel, scale=scale),
        grid=(m // tm,),
        in_specs=[
            pl.BlockSpec((tm, k), lambda i: (i, 0)),
            pl.BlockSpec((tm, d), lambda i: (i, 0)),
            pl.BlockSpec((k, d), lambda i: (0, 0), pipeline_mode=pl.Buffered(1)),
            pl.BlockSpec((1, d), lambda i: (0, 0)),
            pl.BlockSpec((1, d), lambda i: (0, 0)),
        ],
        out_specs=[pl.BlockSpec((tm, d), lambda i: (i, 0))] * n_out,
        out_shape=[jax.ShapeDtypeStruct((m, d), F32), jax.ShapeDtypeStruct((m, d), BF16)][:n_out],
        compiler_params=_cparams(("parallel",)),
        name="proj_ln",
    )(a, x, w, g, b)


def _cross_kernel(x_ref, xb_ref, wq_ref, kv_ref, wo_ref, g_ref, b_ref, o_ref):
    scale = MEM_HEAD_DIM ** -0.5
    q = (_dot(xb_ref[...], wq_ref[...]) * scale).astype(BF16)
    outs = []
    for hh in range(N_MEM_HEADS):
        sl = slice(hh * MEM_HEAD_DIM, (hh + 1) * MEM_HEAD_DIM)
        k = kv_ref[:, sl]
        v = kv_ref[:, MEM_W + hh * MEM_HEAD_DIM:MEM_W + (hh + 1) * MEM_HEAD_DIM]
        s = _dot_nt(q[:, sl], k)
        p = jnp.exp(s - jnp.max(s, axis=-1, keepdims=True))
        p = p / jnp.sum(p, axis=-1, keepdims=True)
        outs.append(_dot(p.astype(BF16), v))
    o_att = jnp.concatenate(outs, axis=1).astype(BF16)
    y = ALPHA * x_ref[...] + _dot(o_att, wo_ref[...])
    o_ref[...] = _layer_norm(y, g_ref[...], b_ref[...])


def _cross(x2, x2b, wq, kv3, wo, g, b, *, seq, tm=512):
    m, d = x2.shape
    tm = min(tm, seq)
    t_per_seq = seq // tm
    mem_len = kv3.shape[1]
    return pl.pallas_call(
        _cross_kernel,
        grid=(m // tm,),
        in_specs=[
            pl.BlockSpec((tm, d), lambda i: (i, 0)),
            pl.BlockSpec((tm, d), lambda i: (i, 0)),
            pl.BlockSpec((d, MEM_W), lambda i: (0, 0)),
            pl.BlockSpec((None, mem_len, 2 * MEM_W), lambda i: (i // t_per_seq, 0, 0)),
            pl.BlockSpec((MEM_W, d), lambda i: (0, 0)),
            pl.BlockSpec((1, d), lambda i: (0, 0)),
            pl.BlockSpec((1, d), lambda i: (0, 0)),
        ],
        out_specs=pl.BlockSpec((tm, d), lambda i: (i, 0)),
        out_shape=jax.ShapeDtypeStruct((m, d), F32),
        compiler_params=_cparams(("parallel",)),
        name="cross_attn",
    )(x2, x2b, wq, kv3, wo, g, b)


def _rope_tables(seq):
    t = np.arange(seq)
    inv_freq = ROPE_THETA ** (-np.arange(ROPE_AXIS_DIM // 2, dtype=np.float64) * 2.0 / ROPE_AXIS_DIM)
    ang_r = (t // GRID_W)[:, None] * inv_freq[None, :]
    ang_c = (t % GRID_W)[:, None] * inv_freq[None, :]
    cr, sr, cc, sc = np.cos(ang_r), np.sin(ang_r), np.cos(ang_c), np.sin(ang_c)
    rc = np.concatenate([cr, cr, cc, cc], axis=1)
    rs = np.concatenate([-sr, sr, -sc, sc], axis=1)
    lane = np.arange(HEAD_DIM)
    swap = (lane[:, None] == (lane[None, :] ^ (ROPE_AXIS_DIM // 2)))
    swap2 = np.concatenate([swap, swap], axis=0)
    return (jnp.asarray(rc.astype(np.float32)), jnp.asarray(rs.astype(np.float32)),
            jnp.asarray(swap2, dtype=BF16))


def kernel(x, mem, ffn1_w1, ffn1_w3, ffn1_w2, ln1_g, ln1_b, w_in, q_norm_g, k_norm_g, conv_w, conv_b,
           dt_bias, a_log, ssd_d, ssd_norm_g, w_branch_attn, w_branch_ssd, w_mix_out, ln2_g, ln2_b,
           mem_wq, mem_wk, mem_wv, mem_wo, ln3_g, ln3_b, ffn2_w1, ffn2_w3, ffn2_w2, ln4_g, ln4_b):
    bsz, seq, d = x.shape
    m = bsz * seq
    mem_len = mem.shape[1]
    bf = lambda t: t.astype(BF16)
    rc, rs, swap = _rope_tables(seq)
    xc = x.reshape(m, d)
    for l in range(DEPTH):
        row = lambda v: v[l].reshape(1, -1)
        later = [ffn1_w2[l], w_in[l], w_branch_attn[l], w_branch_ssd[l], w_mix_out[l],
                 ffn2_w1[l], ffn2_w3[l], ffn2_w2[l]]
        act, f1w2, w_all, w_ba, w_bs, w_mo, f2w1, f2w3, f2w2 = _glu(
            xc, bf(ffn1_w1[l]), bf(ffn1_w3[l]), cast=later)
        x1, x1b = _proj_ln(act, xc, f1w2, row(ln1_g), row(ln1_b), scale=0.5, tm=256)
        dt0 = Q_W + 2 * KV_W + D_INNER + XBC_W
        gains = jnp.concatenate([jnp.tile(q_norm_g[l] * (HEAD_DIM ** -0.5 * LOG2E), N_Q_HEADS),
                                 jnp.tile(k_norm_g[l], N_KV_HEADS)]).reshape(1, -1)
        qk = _qk_proj(x1b, w_all, gains, rc, rs, swap, seq=seq)
        h = _matmul(x1b, w_all, BF16, tm=2048, tn=COL_BLK, n_out=H_W, col0=SRC_ROPE_BLKS * COL_BLK,
                    out_rot=1)
        dt_raw = _matmul(x1b, w_all, F32, tm=1024, tn=DT_W, n_out=DT_W, col0=dt0)
        gate = _matmul(x1b, w_all, BF16, tm=2048, tn=COL_BLK, n_out=2 * D_MODEL, col0=dt0 + DT_W,
                       strips=COL_BLK // LANE)
        h3 = h.reshape(bsz, seq, H_W)
        score_bound = HEAD_DIM * jnp.max(jnp.abs(gains[0, :Q_W])) * jnp.max(jnp.abs(gains[0, Q_W:]))
        y_attn = _attention(qk.reshape(bsz, seq, SRC_ROPE_BLKS * COL_BLK), h3,
                            (score_bound < SCORE_SAFE_LOG2).astype(jnp.int32).reshape(1))
        xbc = _conv_silu(h3, conv_w[l], conv_b[l].reshape(1, -1))
        y_ssd = _ssd(xbc, dt_raw.reshape(bsz, seq, DT_W), dt_bias[l].reshape(1, DT_W),
                     a_log[l].reshape(1, DT_W), h3,
                     jnp.repeat(ssd_d[l], SSD_HEAD_DIM).reshape(1, D_INNER),
                     ssd_norm_g[l].reshape(1, D_INNER))
        merged = _branch_mix(y_attn.reshape(m, Q_W), y_ssd.reshape(m, D_INNER), gate,
                             w_ba, w_bs)
        x2, x2b = _proj_ln(merged, x1, w_mo, row(ln2_g), row(ln2_b))
        w_kv = bf(jnp.concatenate([mem_wk[l], mem_wv[l]], axis=1))
        kv = _matmul(bf(mem.reshape(bsz * mem_len, d)), w_kv, BF16, tm=bsz * mem_len, tn=COL_BLK)
        x3 = _cross(x2, x2b, bf(mem_wq[l]), kv.reshape(bsz, mem_len, 2 * MEM_W), bf(mem_wo[l]),
                       row(ln3_g), row(ln3_b), seq=seq)
        (act,) = _glu(x3, f2w1, f2w3)
        (xc,) = _proj_ln(act, x3, f2w2, row(ln4_g), row(ln4_b), scale=0.5, tm=256,
                         emit_bf16=False)
    return xc.reshape(bsz, seq, d)
```

```python
import functools
import math

import jax
import jax.numpy as jnp
import numpy as np
from jax import lax
from jax.experimental import pallas as pl
from jax.experimental.pallas import tpu as pltpu

F32 = jnp.float32
BF16 = jnp.bfloat16

D_MODEL = 2048
DEPTH = 1
GRID_W = 64
HEAD_DIM = 128
N_Q_HEADS = 16
N_KV_HEADS = 4
GQA_REP = 4
ROPE_AXIS_DIM = 64
ROPE_THETA = 10000.0
D_INNER = 4096
SSD_HEAD_DIM = 64
N_SSD_HEADS = 64
N_SSD_GROUPS = 8
HEADS_PER_GROUP = N_SSD_HEADS // N_SSD_GROUPS
GROUP_W = D_INNER // N_SSD_GROUPS
D_STATE = 128
CONV_W = 5
CHUNK = 128
N_MEM_HEADS = 4
MEM_HEAD_DIM = 128
D_FF = 5632
LN_EPS = 1e-5
RMS_EPS = 1e-6
Q_W = 2048
KV_W = 512
BC_W = 1024
XBC_W = 6144
DT_W = 128
MEM_W = 512
ALPHA = (2.0 * DEPTH) ** 0.25
LOG2E = math.log2(math.e)

LANE = 128
COL_BLK = 512
SRC_ROPE_BLKS = (Q_W + KV_W) // COL_BLK
Z_BLK0 = 0
XBC_BLK0 = Z_BLK0 + D_INNER // COL_BLK
V_BLK = XBC_BLK0 + XBC_W // COL_BLK
H_BLKS = V_BLK + 1
H_W = H_BLKS * COL_BLK

VMEM_LIMIT = 56 * 1024 * 1024


def _cparams(sem):
    return pltpu.CompilerParams(dimension_semantics=sem, vmem_limit_bytes=VMEM_LIMIT)


def _dot(a, b):
    return jnp.dot(a, b, preferred_element_type=F32)


def _dot_nt(a, b):
    return lax.dot_general(a, b, (((1,), (1,)), ((), ())), preferred_element_type=F32)


def _dot_tn(a, b):
    return lax.dot_general(a, b, (((0,), (0,)), ((), ())), preferred_element_type=F32)


def _split3(v):
    hi = v.astype(BF16)
    r1 = v - hi.astype(F32)
    mid = r1.astype(BF16)
    lo = (r1 - mid.astype(F32)).astype(BF16)
    return hi, mid, lo


def _layer_norm(y, g, b):
    mu = jnp.mean(y, axis=-1, keepdims=True)
    d = y - mu
    var = jnp.mean(d * d, axis=-1, keepdims=True)
    return d * lax.rsqrt(var + LN_EPS) * g + b


def _silu(x):
    return x * jax.nn.sigmoid(x)


def _glu_kernel(x_ref, w1_ref, w3_ref, *refs, cast_blocks):
    n_cast = len(cast_blocks)
    cast_in = refs[:n_cast]
    o_ref = refs[n_cast]
    cast_out = refs[n_cast + 1:2 * n_cast + 1]
    xb_ref = refs[-1]

    @pl.when(pl.program_id(1) == 0)
    def _():
        xb_ref[...] = x_ref[...].astype(BF16)

    xb = xb_ref[...]
    o_ref[...] = (_silu(_dot(xb, w1_ref[...])) * _dot(xb, w3_ref[...])).astype(BF16)

    step = pl.program_id(0) * pl.num_programs(1) + pl.program_id(1)
    for src, dst, n_blk in zip(cast_in, cast_out, cast_blocks):
        @pl.when(step < n_blk)
        def _(src=src, dst=dst):
            dst[...] = src[...].astype(BF16)


def _cast_row_block(rows, n_steps):
    bf16_sublanes = 16
    for rb in range(bf16_sublanes, rows + 1, bf16_sublanes):
        if rows % rb == 0 and rows // rb <= n_steps:
            return rb
    raise ValueError((rows, n_steps))


def _glu(x, w1, w3, cast=(), *, tm=1024, tf=512):
    m, d = x.shape
    dff = w1.shape[1]
    tm = min(tm, m)
    n_f = dff // tf
    n_steps = (m // tm) * n_f
    def cast_specs():
        specs = []
        for w in cast:
            rb = _cast_row_block(w.shape[0], n_steps)
            n_blk = w.shape[0] // rb
            specs.append(pl.BlockSpec((rb, w.shape[1]),
                                      lambda i, f, n_blk=n_blk: (jnp.minimum(i * n_f + f, n_blk - 1), 0)))
        return specs

    cast_blocks = [w.shape[0] // _cast_row_block(w.shape[0], n_steps) for w in cast]
    return pl.pallas_call(
        functools.partial(_glu_kernel, cast_blocks=tuple(cast_blocks)),
        grid=(m // tm, n_f),
        in_specs=[
            pl.BlockSpec((tm, d), lambda i, f: (i, 0)),
            pl.BlockSpec((d, tf), lambda i, f: (0, f)),
            pl.BlockSpec((d, tf), lambda i, f: (0, f)),
        ] + cast_specs(),
        out_specs=[pl.BlockSpec((tm, tf), lambda i, f: (i, f))] + cast_specs(),
        out_shape=[jax.ShapeDtypeStruct((m, dff), BF16)]
                  + [jax.ShapeDtypeStruct(w.shape, BF16) for w in cast],
        scratch_shapes=[pltpu.VMEM((tm, d), BF16)],
        compiler_params=_cparams(("arbitrary", "arbitrary")),
        name="glu",
    )(x, w1, w3, *cast)


def _qk_proj_kernel(x_ref, w_ref, gain_ref, rc_ref, rs_ref, swap_ref, o_ref):
    acc = _dot(x_ref[...], w_ref[...])
    rc = rc_ref[...]
    rs = rs_ref[...]
    swap = swap_ref[...]
    for hh in range(COL_BLK // HEAD_DIM):
        sl = slice(hh * HEAD_DIM, (hh + 1) * HEAD_DIM)
        xh = acc[:, sl]
        ms = jnp.mean(xh * xh, axis=-1, keepdims=True)
        xh = xh * lax.rsqrt(ms + RMS_EPS) * gain_ref[:, sl]
        hi = xh.astype(BF16)
        mid = (xh - hi.astype(F32)).astype(BF16)
        partner = _dot(jnp.concatenate([hi, mid], axis=1), swap)
        o_ref[:, sl] = (xh * rc + partner * rs).astype(BF16)


def _qk_proj(xb, w, gains, rc, rs, swap, *, seq, tm=2048):
    m, d = xb.shape
    tm = min(tm, seq)
    t_per_seq = seq // tm
    return pl.pallas_call(
        _qk_proj_kernel,
        grid=(m // tm, SRC_ROPE_BLKS),
        in_specs=[
            pl.BlockSpec((tm, d), lambda i, n: (i, 0)),
            pl.BlockSpec((d, COL_BLK), lambda i, n: (0, n)),
            pl.BlockSpec((1, COL_BLK), lambda i, n: (0, n)),
            pl.BlockSpec((tm, HEAD_DIM), lambda i, n: (i % t_per_seq, 0)),
            pl.BlockSpec((tm, HEAD_DIM), lambda i, n: (i % t_per_seq, 0)),
            pl.BlockSpec((2 * HEAD_DIM, HEAD_DIM), lambda i, n: (0, 0)),
        ],
        out_specs=pl.BlockSpec((tm, COL_BLK), lambda i, n: (i, n)),
        out_shape=jax.ShapeDtypeStruct((m, SRC_ROPE_BLKS * COL_BLK), BF16),
        compiler_params=_cparams(("parallel", "arbitrary")),
        name="qk_proj",
    )(xb, w, gains, rc, rs, swap)


def _matmul_kernel(x_ref, *refs):
    *w_refs, o_ref = refs
    w = w_refs[0][...] if len(w_refs) == 1 else jnp.concatenate([r[...] for r in w_refs], axis=1)
    o_ref[...] = _dot(x_ref[...], w).astype(o_ref.dtype)


def _matmul(xb, w, out_dtype, *, tm, tn, n_out=None, col0=0, out_rot=0, strips=1):
    m, d = xb.shape
    n = w.shape[1] if n_out is None else n_out
    tm = min(tm, m)
    n_blk = n // tn
    ts = tn // strips
    s0 = col0 // ts
    w_spec = lambda s: pl.BlockSpec((d, ts), lambda i, j: (0, s0 + strips * j + s))
    return pl.pallas_call(
        _matmul_kernel,
        grid=(m // tm, n_blk),
        in_specs=[pl.BlockSpec((tm, d), lambda i, j: (i, 0))] + [w_spec(s) for s in range(strips)],
        out_specs=pl.BlockSpec((tm, tn), lambda i, j: (i, (j + n_blk - out_rot) % n_blk)),
        out_shape=jax.ShapeDtypeStruct((m, n), out_dtype),
        compiler_params=_cparams(("parallel", "arbitrary")),
        name="matmul",
    )(xb, *([w] * strips))


SCORE_SAFE_LOG2 = 60.0


def _attn_kernel(safe_ref, q_ref, k_ref, v_ref, o_ref, v1_ref, p_ref, *, tq, tk):
    seq = k_ref.shape[0]

    @pl.when(pl.program_id(2) == 0)
    def _():
        v1_ref[:, :HEAD_DIM] = v_ref[...]
        v1_ref[:, HEAD_DIM:] = jnp.ones((seq, HEAD_DIM), BF16)

    q = jnp.concatenate([q_ref[:, hh * HEAD_DIM:(hh + 1) * HEAD_DIM] for hh in range(GQA_REP)], axis=0)
    rows = GQA_REP * tq

    def finish(acc):
        o = (acc[:, :HEAD_DIM] / acc[:, HEAD_DIM:HEAD_DIM + 1]).astype(BF16)
        for hh in range(GQA_REP):
            o_ref[:, hh * HEAD_DIM:(hh + 1) * HEAD_DIM] = o[hh * tq:(hh + 1) * tq]

    @pl.when(safe_ref[0] == 1)
    def _():
        for j in range(seq // tk):
            s = _dot_nt(q, k_ref[j * tk:(j + 1) * tk, :])
            p_ref[:, j * tk:(j + 1) * tk] = jnp.exp2(s).astype(BF16)
        finish(_dot(p_ref[...], v1_ref[...]))

    @pl.when(safe_ref[0] == 0)
    def _():
        m_i = jnp.full((rows, 1), -jnp.inf, F32)
        acc = jnp.zeros((rows, 2 * HEAD_DIM), F32)
        for j in range(seq // tk):
            s = _dot_nt(q, k_ref[j * tk:(j + 1) * tk, :])
            m_new = jnp.maximum(m_i, jnp.max(s, axis=-1, keepdims=True))
            p = jnp.exp2(s - m_new).astype(BF16)
            acc = jnp.exp2(m_i - m_new) * acc + _dot(p, v1_ref[j * tk:(j + 1) * tk, :])
            m_i = m_new
        finish(acc)


def _attention(qk3, h3, scores_safe, *, tq=512, tk=512):
    bsz, seq, _ = h3.shape
    tq = min(tq, seq)
    tk = min(tk, seq)
    k_blk0 = Q_W // HEAD_DIM
    v_blk0 = V_BLK * COL_BLK // HEAD_DIM
    return pl.pallas_call(
        functools.partial(_attn_kernel, tq=tq, tk=tk),
        grid=(bsz, N_KV_HEADS, seq // tq),
        in_specs=[
            pl.BlockSpec(memory_space=pltpu.SMEM),
            pl.BlockSpec((None, tq, GQA_REP * HEAD_DIM), lambda b, kv, i: (b, i, kv)),
            pl.BlockSpec((None, seq, HEAD_DIM), lambda b, kv, i: (b, 0, k_blk0 + kv)),
            pl.BlockSpec((None, seq, HEAD_DIM), lambda b, kv, i: (b, 0, v_blk0 + kv)),
        ],
        out_specs=pl.BlockSpec((None, tq, GQA_REP * HEAD_DIM), lambda b, kv, i: (b, i, kv)),
        out_shape=jax.ShapeDtypeStruct((bsz, seq, Q_W), BF16),
        scratch_shapes=[pltpu.VMEM((seq, 2 * HEAD_DIM), BF16),
                        pltpu.VMEM((GQA_REP * tq, seq), BF16)],
        compiler_params=_cparams(("parallel", "parallel", "arbitrary")),
        name="attention",
    )(scores_safe, qk3, qk3, h3)


CONV_SUB = 128
CONV_HALO = 64
CONV_PAD = CONV_W // 2
CONV_SIDE_TAPS = tuple(k for k in range(CONV_W) if k != CONV_PAD)


def _conv_kernel(prev_ref, main_ref, next_ref, w_ref, b_ref, shift_ref, o_ref, win_ref, *, rt):
    r = pl.program_id(1)
    first = r == 0
    last = r == pl.num_programs(1) - 1
    prev = prev_ref[...]
    nxt = next_ref[...]
    win_ref[0:CONV_HALO, :] = jnp.where(first, jnp.zeros_like(prev), prev)
    win_ref[CONV_HALO:CONV_HALO + rt, :] = main_ref[...]
    win_ref[CONV_HALO + rt:, :] = jnp.where(last, jnp.zeros_like(nxt), nxt)
    bias = b_ref[...]
    span = CONV_SUB + 2 * CONV_HALO
    for t in range(rt // CONV_SUB):
        win = win_ref[t * CONV_SUB:t * CONV_SUB + span, :]
        centre = win[CONV_HALO:CONV_HALO + CONV_SUB].astype(F32)
        acc = bias + w_ref[CONV_PAD:CONV_PAD + 1, :] * centre
        for j, k in enumerate(CONV_SIDE_TAPS):
            acc = acc + w_ref[k:k + 1, :] * _dot(shift_ref[j], win)
        o_ref[t * CONV_SUB:(t + 1) * CONV_SUB, :] = _silu(acc).astype(BF16)


def _conv_shift_mats():
    i = np.arange(CONV_SUB)[:, None]
    j = np.arange(CONV_SUB + 2 * CONV_HALO)[None, :]
    return jnp.asarray(np.stack([j == i + CONV_HALO + (k - CONV_PAD) for k in CONV_SIDE_TAPS]), dtype=BF16)


def _conv_silu(h3, conv_w, conv_b, *, rt=1024, cw=1024):
    bsz, seq, _ = h3.shape
    rt = min(rt, seq)
    nh = rt // CONV_HALO
    n_halo_blk = seq // CONV_HALO
    span = CONV_SUB + 2 * CONV_HALO
    c0 = XBC_BLK0 * COL_BLK // cw
    return pl.pallas_call(
        functools.partial(_conv_kernel, rt=rt),
        grid=(bsz, seq // rt, XBC_W // cw),
        in_specs=[
            pl.BlockSpec((None, CONV_HALO, cw), lambda b, r, c: (b, jnp.maximum(r * nh - 1, 0), c0 + c)),
            pl.BlockSpec((None, rt, cw), lambda b, r, c: (b, r, c0 + c)),
            pl.BlockSpec((None, CONV_HALO, cw),
                         lambda b, r, c: (b, jnp.minimum((r + 1) * nh, n_halo_blk - 1), c0 + c)),
            pl.BlockSpec((CONV_W, cw), lambda b, r, c: (0, c)),
            pl.BlockSpec((1, cw), lambda b, r, c: (0, c)),
            pl.BlockSpec((len(CONV_SIDE_TAPS), CONV_SUB, span), lambda b, r, c: (0, 0, 0)),
        ],
        out_specs=pl.BlockSpec((None, rt, cw), lambda b, r, c: (b, r, c)),
        out_shape=jax.ShapeDtypeStruct((bsz, seq, XBC_W), BF16),
        scratch_shapes=[pltpu.VMEM((rt + 2 * CONV_HALO, cw), BF16)],
        compiler_params=_cparams(("parallel", "parallel", "arbitrary")),
        name="conv_silu",
    )(h3, h3, h3, conv_w, conv_b, _conv_shift_mats())


def _softplus(x):
    return jnp.maximum(x, 0.0) + jnp.log1p(jnp.exp(-jnp.abs(x)))


def _cumsum_pos(t01, a):
    hi, mid, lo = _split3(a)
    t2 = jnp.concatenate([t01, t01], axis=1)
    return _dot(t2, jnp.concatenate([hi, mid], axis=0)) + _dot(t01, lo)


def _ssd_kernel(*refs, backward):
    if backward:
        (xs_ref, bm_ref, cm_ref, dt_ref, bias_ref, alog_ref, ex_ref, yf_ref, z_ref, dvec_ref, ng_ref,
         o_ref, h_ref) = refs
    else:
        xs_ref, bm_ref, cm_ref, dt_ref, bias_ref, alog_ref, ex_ref, o_ref, h_ref = refs

    @pl.when(pl.program_id(1) == 0)
    def _():
        h_ref[...] = jnp.zeros_like(h_ref)

    lane = lax.broadcasted_iota(jnp.int32, (CHUNK, LANE), 1)
    row = lax.broadcasted_iota(jnp.int32, (CHUNK, LANE), 0)
    nh = N_SSD_HEADS
    hpg = HEADS_PER_GROUP

    dt = _softplus(dt_ref[...] + bias_ref[...])
    a = dt * (-LOG2E * jnp.exp(alog_ref[...]))
    tri_incl = lane <= row
    t_lo = jnp.where(tri_incl, 1.0, 0.0).astype(BF16)
    t_up = jnp.where(lane >= row, 1.0, 0.0).astype(BF16)
    cs = jnp.where(lane < nh, _cumsum_pos(t_lo, a), _cumsum_pos(t_up, a))

    off = nh if backward else 0
    if backward:
        edge = cs[0:1, :]
    else:
        edge = cs[CHUNK - 1:CHUNK, :]
    used = jnp.logical_and(lane >= off, lane < off + nh)
    e_pos = jnp.where(used, jnp.exp2(jnp.where(used, cs, 0.0)), 0.0)
    w_pos = jnp.where(used, jnp.exp2(jnp.where(used, edge - cs, 0.0)) * dt, 0.0)
    st = jnp.concatenate([e_pos, w_pos], axis=0)
    st_hi = st.astype(BF16)
    st_mid = (st - st_hi.astype(F32)).astype(BF16)
    st2 = jnp.concatenate([st_hi, st_mid], axis=1)
    if not backward:
        src_t = (cs - LOG2E * jnp.log(dt)).T

    y_offs, cbs = [], []
    for g in range(N_SSD_GROUPS):
        sl = slice(g * GROUP_W, (g + 1) * GROUP_W)
        sb = slice(g * D_STATE, (g + 1) * D_STATE)
        xs_f = xs_ref[:, sl].astype(F32)
        bm = bm_ref[:, sb]
        cm = cm_ref[:, sb]
        ex = _dot(st2, ex_ref[g])
        e_exp = ex[:CHUNK]
        w_exp = ex[CHUNK:]
        chunk_decay = e_exp[0:1, :] if backward else e_exp[CHUNK - 1:CHUNK, :]
        h = h_ref[g]
        y_off = _dot(cm, h.astype(BF16)) * e_exp
        s_t = _dot_tn(bm, (xs_f * w_exp).astype(BF16))
        h_ref[g] = h * chunk_decay + s_t
        y_offs.append(y_off)
        if not backward:
            cbs.append(_dot_nt(cm, bm))

    for g in range(N_SSD_GROUPS):
        sl = slice(g * GROUP_W, (g + 1) * GROUP_W)
        xs_f = xs_ref[:, sl].astype(F32)
        y_off = y_offs[g]
        if not backward:
            cb = cbs[g]
            pairs = []
            for pp in range(hpg // 2):
                x_pair = xs_f[:, pp * LANE:(pp + 1) * LANE]
                ms = []
                for hh in (g * hpg + 2 * pp, g * hpg + 2 * pp + 1):
                    arg = jnp.where(tri_incl,
                                    cs[:, hh:hh + 1] - src_t[hh:hh + 1, :],
                                    cs[:, nh + hh:nh + hh + 1] - src_t[nh + hh:nh + hh + 1, :])
                    ms.append((cb * jnp.exp2(arg)).astype(BF16))
                x_a = jnp.where(lane < SSD_HEAD_DIM, x_pair, 0.0).astype(BF16)
                x_b = jnp.where(lane >= SSD_HEAD_DIM, x_pair, 0.0).astype(BF16)
                pairs.append(_dot(jnp.concatenate(ms, axis=1), jnp.concatenate([x_a, x_b], axis=0)))
            o_ref[:, sl] = jnp.concatenate(pairs, axis=1) + y_off
        else:
            y = yf_ref[:, sl] + y_off + dvec_ref[:, sl] * xs_f
            y = y * _silu(z_ref[:, sl].astype(F32))
            y = y * lax.rsqrt(jnp.mean(y * y, axis=-1, keepdims=True) + RMS_EPS) * ng_ref[:, sl]
            o_ref[:, sl] = y.astype(BF16)


def _expand_mats(off):
    k = (np.arange(2 * LANE) % LANE)[None, :, None]
    col = (np.arange(GROUP_W) // SSD_HEAD_DIM)[None, None, :]
    g = np.arange(N_SSD_GROUPS)[:, None, None]
    return jnp.asarray(k == off + HEADS_PER_GROUP * g + col, dtype=BF16)


def _ssd(xbc3, dt3, bias, alog, h3, dvec, ng):
    bsz, seq, _ = xbc3.shape
    nc = seq // CHUNK
    g_n = N_SSD_GROUPS
    scratch = [pltpu.VMEM((g_n, D_STATE, GROUP_W), F32)]
    sem = ("parallel", "arbitrary")

    def common_specs(cidx):
        return [
            pl.BlockSpec((None, CHUNK, D_INNER), lambda b, j: (b, cidx(j), 0)),
            pl.BlockSpec((None, CHUNK, BC_W), lambda b, j: (b, cidx(j), D_INNER // BC_W)),
            pl.BlockSpec((None, CHUNK, BC_W), lambda b, j: (b, cidx(j), D_INNER // BC_W + 1)),
            pl.BlockSpec((None, CHUNK, LANE), lambda b, j: (b, cidx(j), 0)),
            pl.BlockSpec((1, LANE), lambda b, j: (0, 0)),
            pl.BlockSpec((1, LANE), lambda b, j: (0, 0)),
            pl.BlockSpec((g_n, 2 * LANE, GROUP_W), lambda b, j: (0, 0, 0)),
        ]

    fwd = lambda j: j
    y_f = pl.pallas_call(
        functools.partial(_ssd_kernel, backward=False),
        grid=(bsz, nc),
        in_specs=common_specs(fwd),
        out_specs=pl.BlockSpec((None, CHUNK, D_INNER), lambda b, j: (b, j, 0)),
        out_shape=jax.ShapeDtypeStruct((bsz, seq, D_INNER), F32),
        scratch_shapes=scratch,
        compiler_params=_cparams(sem),
        name="ssd_fwd",
    )(xbc3, xbc3, xbc3, dt3, bias, alog, _expand_mats(0))

    bwd = lambda j: nc - 1 - j
    return pl.pallas_call(
        functools.partial(_ssd_kernel, backward=True),
        grid=(bsz, nc),
        in_specs=common_specs(bwd) + [
            pl.BlockSpec((None, CHUNK, D_INNER), lambda b, j: (b, bwd(j), 0)),
            pl.BlockSpec((None, CHUNK, D_INNER), lambda b, j: (b, bwd(j), Z_BLK0 * COL_BLK // D_INNER)),
            pl.BlockSpec((1, D_INNER), lambda b, j: (0, 0)),
            pl.BlockSpec((1, D_INNER), lambda b, j: (0, 0)),
        ],
        out_specs=pl.BlockSpec((None, CHUNK, D_INNER), lambda b, j: (b, bwd(j), 0)),
        out_shape=jax.ShapeDtypeStruct((bsz, seq, D_INNER), BF16),
        scratch_shapes=scratch,
        compiler_params=_cparams(sem),
        name="ssd_bwd",
    )(xbc3, xbc3, xbc3, dt3, bias, alog, _expand_mats(N_SSD_HEADS), y_f, h3, dvec, ng)


def _branch_mix_kernel(ya_ref, ys_ref, ga_ref, gs_ref, wa_ref, ws_ref, o_ref):
    pa = jax.nn.sigmoid(ga_ref[...].astype(F32)) * _dot(ya_ref[...], wa_ref[...])
    ps = jax.nn.sigmoid(gs_ref[...].astype(F32)) * _dot(ys_ref[...], ws_ref[...])
    o_ref[...] = (pa + ps).astype(BF16)


def _branch_mix(y_attn, y_ssd, gate, wa, ws, *, tm=1024, tn=512):
    m = y_attn.shape[0]
    d = wa.shape[1]
    tm = min(tm, m)
    n_blk = d // tn
    return pl.pallas_call(
        _branch_mix_kernel,
        grid=(m // tm, n_blk),
        in_specs=[
            pl.BlockSpec((tm, Q_W), lambda i, n: (i, 0)),
            pl.BlockSpec((tm, D_INNER), lambda i, n: (i, 0)),
            pl.BlockSpec((tm, tn), lambda i, n: (i, n)),
            pl.BlockSpec((tm, tn), lambda i, n: (i, n_blk + n)),
            pl.BlockSpec((Q_W, tn), lambda i, n: (0, n)),
            pl.BlockSpec((D_INNER, tn), lambda i, n: (0, n)),
        ],
        out_specs=pl.BlockSpec((tm, tn), lambda i, n: (i, n)),
        out_shape=jax.ShapeDtypeStruct((m, d), BF16),
        compiler_params=_cparams(("parallel", "arbitrary")),
        name="branch_mix",
    )(y_attn, y_ssd, gate, gate, wa, ws)


def _proj_ln_kernel(a_ref, x_ref, w_ref, g_ref, b_ref, o_ref, *maybe_ob_ref, scale):
    y = _dot(a_ref[...], w_ref[...])
    if scale != 1.0:
        y = scale * y
    o = _layer_norm(ALPHA * x_ref[...] + y, g_ref[...], b_ref[...])
    o_ref[...] = o
    for ob_ref in maybe_ob_ref:
        ob_ref[...] = o.astype(BF16)


def _proj_ln(a, x, w, g, b, *, scale=1.0, tm=512, emit_bf16=True):
    m, d = x.shape
    k = a.shape[1]
    tm = min(tm, m)
    n_out = 2 if emit_bf16 else 1
    return pl.pallas_call(
        functools.partial(_proj_ln_kernel, scale=scale),
        grid=(m // tm,),
        in_specs=[
            pl.BlockSpec((tm, k), lambda i: (i, 0)),
            pl.BlockSpec((tm, d), lambda i: (i, 0)),
            pl.BlockSpec((k, d), lambda i: (0, 0), pipeline_mode=pl.Buffered(1)),
            pl.BlockSpec((1, d), lambda i: (0, 0)),
            pl.BlockSpec((1, d), lambda i: (0, 0)),
        ],
        out_specs=[pl.BlockSpec((tm, d), lambda i: (i, 0))] * n_out,
        out_shape=[jax.ShapeDtypeStruct((m, d), F32), jax.ShapeDtypeStruct((m, d), BF16)][:n_out],
        compiler_params=_cparams(("parallel",)),
        name="proj_ln",
    )(a, x, w, g, b)


def _cross_kernel(x_ref, xb_ref, wq_ref, kv_ref, wo_ref, g_ref, b_ref, o_ref):
    scale = MEM_HEAD_DIM ** -0.5
    q = (_dot(xb_ref[...], wq_ref[...]) * scale).astype(BF16)
    outs = []
    for hh in range(N_MEM_HEADS):
        sl = slice(hh * MEM_HEAD_DIM, (hh + 1) * MEM_HEAD_DIM)
        k = kv_ref[:, sl]
        v = kv_ref[:, MEM_W + hh * MEM_HEAD_DIM:MEM_W + (hh + 1) * MEM_HEAD_DIM]
        s = _dot_nt(q[:, sl], k)
        p = jnp.exp(s - jnp.max(s, axis=-1, keepdims=True))
        p = p / jnp.sum(p, axis=-1, keepdims=True)
        outs.append(_dot(p.astype(BF16), v))
    o_att = jnp.concatenate(outs, axis=1).astype(BF16)
    y = ALPHA * x_ref[...] + _dot(o_att, wo_ref[...])
    o_ref[...] = _layer_norm(y, g_ref[...], b_ref[...])


def _cross(x2, x2b, wq, kv3, wo, g, b, *, seq, tm=512):
    m, d = x2.shape
    tm = min(tm, seq)
    t_per_seq = seq // tm
    mem_len = kv3.shape[1]
    return pl.pallas_call(
        _cross_kernel,
        grid=(m // tm,),
        in_specs=[
            pl.BlockSpec((tm, d), lambda i: (i, 0)),
            pl.BlockSpec((tm, d), lambda i: (i, 0)),
            pl.BlockSpec((d, MEM_W), lambda i: (0, 0)),
            pl.BlockSpec((None, mem_len, 2 * MEM_W), lambda i: (i // t_per_seq, 0, 0)),
            pl.BlockSpec((MEM_W, d), lambda i: (0, 0)),
            pl.BlockSpec((1, d), lambda i: (0, 0)),
            pl.BlockSpec((1, d), lambda i: (0, 0)),
        ],
        out_specs=pl.BlockSpec((tm, d), lambda i: (i, 0)),
        out_shape=jax.ShapeDtypeStruct((m, d), F32),
        compiler_params=_cparams(("parallel",)),
        name="cross_attn",
    )(x2, x2b, wq, kv3, wo, g, b)


def _rope_tables(seq):
    t = np.arange(seq)
    inv_freq = ROPE_THETA ** (-np.arange(ROPE_AXIS_DIM // 2, dtype=np.float64) * 2.0 / ROPE_AXIS_DIM)
    ang_r = (t // GRID_W)[:, None] * inv_freq[None, :]
    ang_c = (t % GRID_W)[:, None] * inv_freq[None, :]
    cr, sr, cc, sc = np.cos(ang_r), np.sin(ang_r), np.cos(ang_c), np.sin(ang_c)
    rc = np.concatenate([cr, cr, cc, cc], axis=1)
    rs = np.concatenate([-sr, sr, -sc, sc], axis=1)
    lane = np.arange(HEAD_DIM)
    swap = (lane[:, None] == (lane[None, :] ^ (ROPE_AXIS_DIM // 2)))
    swap2 = np.concatenate([swap, swap], axis=0)
    return (jnp.asarray(rc.astype(np.float32)), jnp.asarray(rs.astype(np.float32)),
            jnp.asarray(swap2, dtype=BF16))


def kernel(x, mem, ffn1_w1, ffn1_w3, ffn1_w2, ln1_g, ln1_b, w_in, q_norm_g, k_norm_g, conv_w, conv_b,
           dt_bias, a_log, ssd_d, ssd_norm_g, w_branch_attn, w_branch_ssd, w_mix_out, ln2_g, ln2_b,
           mem_wq, mem_wk, mem_wv, mem_wo, ln3_g, ln3_b, ffn2_w1, ffn2_w3, ffn2_w2, ln4_g, ln4_b):
    bsz, seq, d = x.shape
    m = bsz * seq
    mem_len = mem.shape[1]
    bf = lambda t: t.astype(BF16)
    rc, rs, swap = _rope_tables(seq)
    xc = x.reshape(m, d)
    for l in range(DEPTH):
        row = lambda v: v[l].reshape(1, -1)
        later = [ffn1_w2[l], w_in[l], w_branch_attn[l], w_branch_ssd[l], w_mix_out[l],
                 ffn2_w1[l], ffn2_w3[l], ffn2_w2[l]]
        act, f1w2, w_all, w_ba, w_bs, w_mo, f2w1, f2w3, f2w2 = _glu(
            xc, bf(ffn1_w1[l]), bf(ffn1_w3[l]), cast=later)
        x1, x1b = _proj_ln(act, xc, f1w2, row(ln1_g), row(ln1_b), scale=0.5, tm=256)
        dt0 = Q_W + 2 * KV_W + D_INNER + XBC_W
        gains = jnp.concatenate([jnp.tile(q_norm_g[l] * (HEAD_DIM ** -0.5 * LOG2E), N_Q_HEADS),
                                 jnp.tile(k_norm_g[l], N_KV_HEADS)]).reshape(1, -1)
        qk = _qk_proj(x1b, w_all, gains, rc, rs, swap, seq=seq)
        h = _matmul(x1b, w_all, BF16, tm=2048, tn=COL_BLK, n_out=H_W, col0=SRC_ROPE_BLKS * COL_BLK,
                    out_rot=1)
        dt_raw = _matmul(x1b, w_all, F32, tm=1024, tn=DT_W, n_out=DT_W, col0=dt0)
        gate = _matmul(x1b, w_all, BF16, tm=2048, tn=COL_BLK, n_out=2 * D_MODEL, col0=dt0 + DT_W,
                       strips=COL_BLK // LANE)
        h3 = h.reshape(bsz, seq, H_W)
        score_bound = HEAD_DIM * jnp.max(jnp.abs(gains[0, :Q_W])) * jnp.max(jnp.abs(gains[0, Q_W:]))
        y_attn = _attention(qk.reshape(bsz, seq, SRC_ROPE_BLKS * COL_BLK), h3,
                            (score_bound < SCORE_SAFE_LOG2).astype(jnp.int32).reshape(1))
        xbc = _conv_silu(h3, conv_w[l], conv_b[l].reshape(1, -1))
        y_ssd = _ssd(xbc, dt_raw.reshape(bsz, seq, DT_W), dt_bias[l].reshape(1, DT_W),
                     a_log[l].reshape(1, DT_W), h3,
                     jnp.repeat(ssd_d[l], SSD_HEAD_DIM).reshape(1, D_INNER),
                     ssd_norm_g[l].reshape(1, D_INNER))
        merged = _branch_mix(y_attn.reshape(m, Q_W), y_ssd.reshape(m, D_INNER), gate,
                             w_ba, w_bs)
        x2, x2b = _proj_ln(merged, x1, w_mo, row(ln2_g), row(ln2_b))
        w_kv = bf(jnp.concatenate([mem_wk[l], mem_wv[l]], axis=1))
        kv = _matmul(bf(mem.reshape(bsz * mem_len, d)), w_kv, BF16, tm=bsz * mem_len, tn=COL_BLK)
        x3 = _cross(x2, x2b, bf(mem_wq[l]), kv.reshape(bsz, mem_len, 2 * MEM_W), bf(mem_wo[l]),
                       row(ln3_g), row(ln3_b), seq=seq)
        (act,) = _glu(x3, f2w1, f2w3)
        (xc,) = _proj_ln(act, x3, f2w2, row(ln4_g), row(ln4_b), scale=0.5, tm=256,
                         emit_bf16=False)
    return xc.reshape(bsz, seq, d)
```

```python
import functools
import math

import jax
import jax.numpy as jnp
import numpy as np
from jax import lax
from jax.experimental import pallas as pl
from jax.experimental.pallas import tpu as pltpu

F32 = jnp.float32
BF16 = jnp.bfloat16

D_MODEL = 2048
DEPTH = 1
GRID_W = 64
HEAD_DIM = 128
N_Q_HEADS = 16
N_KV_HEADS = 4
GQA_REP = 4
ROPE_AXIS_DIM = 64
ROPE_THETA = 10000.0
D_INNER = 4096
SSD_HEAD_DIM = 64
N_SSD_HEADS = 64
N_SSD_GROUPS = 8
HEADS_PER_GROUP = N_SSD_HEADS // N_SSD_GROUPS
GROUP_W = D_INNER // N_SSD_GROUPS
D_STATE = 128
CONV_W = 5
CHUNK = 128
N_MEM_HEADS = 4
MEM_HEAD_DIM = 128
D_FF = 5632
LN_EPS = 1e-5
RMS_EPS = 1e-6
Q_W = 2048
KV_W = 512
BC_W = 1024
XBC_W = 6144
DT_W = 128
MEM_W = 512
ALPHA = (2.0 * DEPTH) ** 0.25
LOG2E = math.log2(math.e)

LANE = 128
COL_BLK = 512
SRC_ROPE_BLKS = (Q_W + KV_W) // COL_BLK
Z_BLK0 = 0
XBC_BLK0 = Z_BLK0 + D_INNER // COL_BLK
V_BLK = XBC_BLK0 + XBC_W // COL_BLK
H_BLKS = V_BLK + 1
H_W = H_BLKS * COL_BLK

VMEM_LIMIT = 56 * 1024 * 1024


def _cparams(sem):
    return pltpu.CompilerParams(dimension_semantics=sem, vmem_limit_bytes=VMEM_LIMIT)


def _dot(a, b):
    return jnp.dot(a, b, preferred_element_type=F32)


def _dot_nt(a, b):
    return lax.dot_general(a, b, (((1,), (1,)), ((), ())), preferred_element_type=F32)


def _dot_tn(a, b):
    return lax.dot_general(a, b, (((0,), (0,)), ((), ())), preferred_element_type=F32)


def _split3(v):
    hi = v.astype(BF16)
    r1 = v - hi.astype(F32)
    mid = r1.astype(BF16)
    lo = (r1 - mid.astype(F32)).astype(BF16)
    return hi, mid, lo


def _layer_norm(y, g, b):
    mu = jnp.mean(y, axis=-1, keepdims=True)
    d = y - mu
    var = jnp.mean(d * d, axis=-1, keepdims=True)
    return d * lax.rsqrt(var + LN_EPS) * g + b


def _silu(x):
    return x * jax.nn.sigmoid(x)


def _glu_kernel(x_ref, w1_ref, w3_ref, *refs, cast_blocks):
    n_cast = len(cast_blocks)
    cast_in = refs[:n_cast]
    o_ref = refs[n_cast]
    cast_out = refs[n_cast + 1:2 * n_cast + 1]
    xb_ref = refs[-1]

    @pl.when(pl.program_id(1) == 0)
    def _():
        xb_ref[...] = x_ref[...].astype(BF16)

    xb = xb_ref[...]
    o_ref[...] = (_silu(_dot(xb, w1_ref[...])) * _dot(xb, w3_ref[...])).astype(BF16)

    step = pl.program_id(0) * pl.num_programs(1) + pl.program_id(1)
    for src, dst, n_blk in zip(cast_in, cast_out, cast_blocks):
        @pl.when(step < n_blk)
        def _(src=src, dst=dst):
            dst[...] = src[...].astype(BF16)


def _cast_row_block(rows, n_steps):
    bf16_sublanes = 16
    for rb in range(bf16_sublanes, rows + 1, bf16_sublanes):
        if rows % rb == 0 and rows // rb <= n_steps:
            return rb
    raise ValueError((rows, n_steps))


def _glu(x, w1, w3, cast=(), *, tm=1024, tf=512):
    m, d = x.shape
    dff = w1.shape[1]
    tm = min(tm, m)
    n_f = dff // tf
    n_steps = (m // tm) * n_f
    def cast_specs():
        specs = []
        for w in cast:
            rb = _cast_row_block(w.shape[0], n_steps)
            n_blk = w.shape[0] // rb
            specs.append(pl.BlockSpec((rb, w.shape[1]),
                                      lambda i, f, n_blk=n_blk: (jnp.minimum(i * n_f + f, n_blk - 1), 0)))
        return specs

    cast_blocks = [w.shape[0] // _cast_row_block(w.shape[0], n_steps) for w in cast]
    return pl.pallas_call(
        functools.partial(_glu_kernel, cast_blocks=tuple(cast_blocks)),
        grid=(m // tm, n_f),
        in_specs=[
            pl.BlockSpec((tm, d), lambda i, f: (i, 0)),
            pl.BlockSpec((d, tf), lambda i, f: (0, f)),
            pl.BlockSpec((d, tf), lambda i, f: (0, f)),
        ] + cast_specs(),
        out_specs=[pl.BlockSpec((tm, tf), lambda i, f: (i, f))] + cast_specs(),
        out_shape=[jax.ShapeDtypeStruct((m, dff), BF16)]
                  + [jax.ShapeDtypeStruct(w.shape, BF16) for w in cast],
        scratch_shapes=[pltpu.VMEM((tm, d), BF16)],
        compiler_params=_cparams(("arbitrary", "arbitrary")),
        name="glu",
    )(x, w1, w3, *cast)


def _qk_proj_kernel(x_ref, w_ref, gain_ref, rc_ref, rs_ref, swap_ref, o_ref):
    acc = _dot(x_ref[...], w_ref[...])
    rc = rc_ref[...]
    rs = rs_ref[...]
    swap = swap_ref[...]
    for hh in range(COL_BLK // HEAD_DIM):
        sl = slice(hh * HEAD_DIM, (hh + 1) * HEAD_DIM)
        xh = acc[:, sl]
        ms = jnp.mean(xh * xh, axis=-1, keepdims=True)
        xh = xh * lax.rsqrt(ms + RMS_EPS) * gain_ref[:, sl]
        hi = xh.astype(BF16)
        mid = (xh - hi.astype(F32)).astype(BF16)
        partner = _dot(jnp.concatenate([hi, mid], axis=1), swap)
        o_ref[:, sl] = (xh * rc + partner * rs).astype(BF16)


def _qk_proj(xb, w, gains, rc, rs, swap, *, seq, tm=2048):
    m, d = xb.shape
    tm = min(tm, seq)
    t_per_seq = seq // tm
    return pl.pallas_call(
        _qk_proj_kernel,
        grid=(m // tm, SRC_ROPE_BLKS),
        in_specs=[
            pl.BlockSpec((tm, d), lambda i, n: (i, 0)),
            pl.BlockSpec((d, COL_BLK), lambda i, n: (0, n)),
            pl.BlockSpec((1, COL_BLK), lambda i, n: (0, n)),
            pl.BlockSpec((tm, HEAD_DIM), lambda i, n: (i % t_per_seq, 0)),
            pl.BlockSpec((tm, HEAD_DIM), lambda i, n: (i % t_per_seq, 0)),
            pl.BlockSpec((2 * HEAD_DIM, HEAD_DIM), lambda i, n: (0, 0)),
        ],
        out_specs=pl.BlockSpec((tm, COL_BLK), lambda i, n: (i, n)),
        out_shape=jax.ShapeDtypeStruct((m, SRC_ROPE_BLKS * COL_BLK), BF16),
        compiler_params=_cparams(("parallel", "arbitrary")),
        name="qk_proj",
    )(xb, w, gains, rc, rs, swap)


def _matmul_kernel(x_ref, *refs):
    *w_refs, o_ref = refs
    w = w_refs[0][...] if len(w_refs) == 1 else jnp.concatenate([r[...] for r in w_refs], axis=1)
    o_ref[...] = _dot(x_ref[...], w).astype(o_ref.dtype)


def _matmul(xb, w, out_dtype, *, tm, tn, n_out=None, col0=0, out_rot=0, strips=1):
    m, d = xb.shape
    n = w.shape[1] if n_out is None else n_out
    tm = min(tm, m)
    n_blk = n // tn
    ts = tn // strips
    s0 = col0 // ts
    w_spec = lambda s: pl.BlockSpec((d, ts), lambda i, j: (0, s0 + strips * j + s))
    return pl.pallas_call(
        _matmul_kernel,
        grid=(m // tm, n_blk),
        in_specs=[pl.BlockSpec((tm, d), lambda i, j: (i, 0))] + [w_spec(s) for s in range(strips)],
        out_specs=pl.BlockSpec((tm, tn), lambda i, j: (i, (j + n_blk - out_rot) % n_blk)),
        out_shape=jax.ShapeDtypeStruct((m, n), out_dtype),
        compiler_params=_cparams(("parallel", "arbitrary")),
        name="matmul",
    )(xb, *([w] * strips))


SCORE_SAFE_LOG2 = 60.0


def _attn_kernel(safe_ref, q_ref, k_ref, v_ref, o_ref, v1_ref, p_ref, *, tq, tk):
    seq = k_ref.shape[0]

    @pl.when(pl.program_id(2) == 0)
    def _():
        v1_ref[:, :HEAD_DIM] = v_ref[...]
        v1_ref[:, HEAD_DIM:] = jnp.ones((seq, HEAD_DIM), BF16)

    q = jnp.concatenate([q_ref[:, hh * HEAD_DIM:(hh + 1) * HEAD_DIM] for hh in range(GQA_REP)], axis=0)
    rows = GQA_REP * tq

    def finish(acc):
        o = (acc[:, :HEAD_DIM] / acc[:, HEAD_DIM:HEAD_DIM + 1]).astype(BF16)
        for hh in range(GQA_REP):
            o_ref[:, hh * HEAD_DIM:(hh + 1) * HEAD_DIM] = o[hh * tq:(hh + 1) * tq]

    @pl.when(safe_ref[0] == 1)
    def _():
        for j in range(seq // tk):
            s = _dot_nt(q, k_ref[j * tk:(j + 1) * tk, :])
            p_ref[:, j * tk:(j + 1) * tk] = jnp.exp2(s).astype(BF16)
        finish(_dot(p_ref[...], v1_ref[...]))

    @pl.when(safe_ref[0] == 0)
    def _():
        m_i = jnp.full((rows, 1), -jnp.inf, F32)
        acc = jnp.zeros((rows, 2 * HEAD_DIM), F32)
        for j in range(seq // tk):
            s = _dot_nt(q, k_ref[j * tk:(j + 1) * tk, :])
            m_new = jnp.maximum(m_i, jnp.max(s, axis=-1, keepdims=True))
            p = jnp.exp2(s - m_new).astype(BF16)
            acc = jnp.exp2(m_i - m_new) * acc + _dot(p, v1_ref[j * tk:(j + 1) * tk, :])
            m_i = m_new
        finish(acc)


def _attention(qk3, h3, scores_safe, *, tq=512, tk=512):
    bsz, seq, _ = h3.shape
    tq = min(tq, seq)
    tk = min(tk, seq)
    k_blk0 = Q_W // HEAD_DIM
    v_blk0 = V_BLK * COL_BLK // HEAD_DIM
    return pl.pallas_call(
        functools.partial(_attn_kernel, tq=tq, tk=tk),
        grid=(bsz, N_KV_HEADS, seq // tq),
        in_specs=[
            pl.BlockSpec(memory_space=pltpu.SMEM),
            pl.BlockSpec((None, tq, GQA_REP * HEAD_DIM), lambda b, kv, i: (b, i, kv)),
            pl.BlockSpec((None, seq, HEAD_DIM), lambda b, kv, i: (b, 0, k_blk0 + kv)),
            pl.BlockSpec((None, seq, HEAD_DIM), lambda b, kv, i: (b, 0, v_blk0 + kv)),
        ],
        out_specs=pl.BlockSpec((None, tq, GQA_REP * HEAD_DIM), lambda b, kv, i: (b, i, kv)),
        out_shape=jax.ShapeDtypeStruct((bsz, seq, Q_W), BF16),
        scratch_shapes=[pltpu.VMEM((seq, 2 * HEAD_DIM), BF16),
                        pltpu.VMEM((GQA_REP * tq, seq), BF16)],
        compiler_params=_cparams(("parallel", "parallel", "arbitrary")),
        name="attention",
    )(scores_safe, qk3, qk3, h3)


CONV_SUB = 128
CONV_HALO = 64
CONV_PAD = CONV_W // 2
CONV_SIDE_TAPS = tuple(k for k in range(CONV_W) if k != CONV_PAD)


def _conv_kernel(prev_ref, main_ref, next_ref, w_ref, b_ref, shift_ref, o_ref, win_ref, *, rt):
    r = pl.program_id(1)
    first = r == 0
    last = r == pl.num_programs(1) - 1
    prev = prev_ref[...]
    nxt = next_ref[...]
    win_ref[0:CONV_HALO, :] = jnp.where(first, jnp.zeros_like(prev), prev)
    win_ref[CONV_HALO:CONV_HALO + rt, :] = main_ref[...]
    win_ref[CONV_HALO + rt:, :] = jnp.where(last, jnp.zeros_like(nxt), nxt)
    bias = b_ref[...]
    span = CONV_SUB + 2 * CONV_HALO
    for t in range(rt // CONV_SUB):
        win = win_ref[t * CONV_SUB:t * CONV_SUB + span, :]
        centre = win[CONV_HALO:CONV_HALO + CONV_SUB].astype(F32)
        acc = bias + w_ref[CONV_PAD:CONV_PAD + 1, :] * centre
        for j, k in enumerate(CONV_SIDE_TAPS):
            acc = acc + w_ref[k:k + 1, :] * _dot(shift_ref[j], win)
        o_ref[t * CONV_SUB:(t + 1) * CONV_SUB, :] = _silu(acc).astype(BF16)


def _conv_shift_mats():
    i = np.arange(CONV_SUB)[:, None]
    j = np.arange(CONV_SUB + 2 * CONV_HALO)[None, :]
    return jnp.asarray(np.stack([j == i + CONV_HALO + (k - CONV_PAD) for k in CONV_SIDE_TAPS]), dtype=BF16)


def _conv_silu(h3, conv_w, conv_b, *, rt=1024, cw=1024):
    bsz, seq, _ = h3.shape
    rt = min(rt, seq)
    nh = rt // CONV_HALO
    n_halo_blk = seq // CONV_HALO
    span = CONV_SUB + 2 * CONV_HALO
    c0 = XBC_BLK0 * COL_BLK // cw
    return pl.pallas_call(
        functools.partial(_conv_kernel, rt=rt),
        grid=(bsz, seq // rt, XBC_W // cw),
        in_specs=[
            pl.BlockSpec((None, CONV_HALO, cw), lambda b, r, c: (b, jnp.maximum(r * nh - 1, 0), c0 + c)),
            pl.BlockSpec((None, rt, cw), lambda b, r, c: (b, r, c0 + c)),
            pl.BlockSpec((None, CONV_HALO, cw),
                         lambda b, r, c: (b, jnp.minimum((r + 1) * nh, n_halo_blk - 1), c0 + c)),
            pl.BlockSpec((CONV_W, cw), lambda b, r, c: (0, c)),
            pl.BlockSpec((1, cw), lambda b, r, c: (0, c)),
            pl.BlockSpec((len(CONV_SIDE_TAPS), CONV_SUB, span), lambda b, r, c: (0, 0, 0)),
        ],
        out_specs=pl.BlockSpec((None, rt, cw), lambda b, r, c: (b, r, c)),
        out_shape=jax.ShapeDtypeStruct((bsz, seq, XBC_W), BF16),
        scratch_shapes=[pltpu.VMEM((rt + 2 * CONV_HALO, cw), BF16)],
        compiler_params=_cparams(("parallel", "parallel", "arbitrary")),
        name="conv_silu",
    )(h3, h3, h3, conv_w, conv_b, _conv_shift_mats())


def _softplus(x):
    return jnp.maximum(x, 0.0) + jnp.log1p(jnp.exp(-jnp.abs(x)))


def _cumsum_pos(t01, a):
    hi, mid, lo = _split3(a)
    t2 = jnp.concatenate([t01, t01], axis=1)
    return _dot(t2, jnp.concatenate([hi, mid], axis=0)) + _dot(t01, lo)


def _ssd_kernel(*refs, backward):
    if backward:
        (xs_ref, bm_ref, cm_ref, dt_ref, bias_ref, alog_ref, ex_ref, yf_ref, z_ref, dvec_ref, ng_ref,
         o_ref, h_ref) = refs
    else:
        xs_ref, bm_ref, cm_ref, dt_ref, bias_ref, alog_ref, ex_ref, o_ref, h_ref = refs

    @pl.when(pl.program_id(1) == 0)
    def _():
        h_ref[...] = jnp.zeros_like(h_ref)

    lane = lax.broadcasted_iota(jnp.int32, (CHUNK, LANE), 1)
    row = lax.broadcasted_iota(jnp.int32, (CHUNK, LANE), 0)
    nh = N_SSD_HEADS
    hpg = HEADS_PER_GROUP

    dt = _softplus(dt_ref[...] + bias_ref[...])
    a = dt * (-LOG2E * jnp.exp(alog_ref[...]))
    tri_incl = lane <= row
    t_lo = jnp.where(tri_incl, 1.0, 0.0).astype(BF16)
    t_up = jnp.where(lane >= row, 1.0, 0.0).astype(BF16)
    cs = jnp.where(lane < nh, _cumsum_pos(t_lo, a), _cumsum_pos(t_up, a))

    off = nh if backward else 0
    if backward:
        edge = cs[0:1, :]
    else:
        edge = cs[CHUNK - 1:CHUNK, :]
    used = jnp.logical_and(lane >= off, lane < off + nh)
    e_pos = jnp.where(used, jnp.exp2(jnp.where(used, cs, 0.0)), 0.0)
    w_pos = jnp.where(used, jnp.exp2(jnp.where(used, edge - cs, 0.0)) * dt, 0.0)
    st = jnp.concatenate([e_pos, w_pos], axis=0)
    st_hi = st.astype(BF16)
    st_mid = (st - st_hi.astype(F32)).astype(BF16)
    st2 = jnp.concatenate([st_hi, st_mid], axis=1)
    if not backward:
        src_t = (cs - LOG2E * jnp.log(dt)).T

    y_offs, cbs = [], []
    for g in range(N_SSD_GROUPS):
        sl = slice(g * GROUP_W, (g + 1) * GROUP_W)
        sb = slice(g * D_STATE, (g + 1) * D_STATE)
        xs_f = xs_ref[:, sl].astype(F32)
        bm = bm_ref[:, sb]
        cm = cm_ref[:, sb]
        ex = _dot(st2, ex_ref[g])
        e_exp = ex[:CHUNK]
        w_exp = ex[CHUNK:]
        chunk_decay = e_exp[0:1, :] if backward else e_exp[CHUNK - 1:CHUNK, :]
        h = h_ref[g]
        y_off = _dot(cm, h.astype(BF16)) * e_exp
        s_t = _dot_tn(bm, (xs_f * w_exp).astype(BF16))
        h_ref[g] = h * chunk_decay + s_t
        y_offs.append(y_off)
        if not backward:
            cbs.append(_dot_nt(cm, bm))

    for g in range(N_SSD_GROUPS):
        sl = slice(g * GROUP_W, (g + 1) * GROUP_W)
        xs_f = xs_ref[:, sl].astype(F32)
        y_off = y_offs[g]
        if not backward:
            cb = cbs[g]
            pairs = []
            for pp in range(hpg // 2):
                x_pair = xs_f[:, pp * LANE:(pp + 1) * LANE]
                ms = []
                for hh in (g * hpg + 2 * pp, g * hpg + 2 * pp + 1):
                    arg = jnp.where(tri_incl,
                                    cs[:, hh:hh + 1] - src_t[hh:hh + 1, :],
                                    cs[:, nh + hh:nh + hh + 1] - src_t[nh + hh:nh + hh + 1, :])
                    ms.append((cb * jnp.exp2(arg)).astype(BF16))
                x_a = jnp.where(lane < SSD_HEAD_DIM, x_pair, 0.0).astype(BF16)
                x_b = jnp.where(lane >= SSD_HEAD_DIM, x_pair, 0.0).astype(BF16)
                pairs.append(_dot(jnp.concatenate(ms, axis=1), jnp.concatenate([x_a, x_b], axis=0)))
            o_ref[:, sl] = jnp.concatenate(pairs, axis=1) + y_off
        else:
            y = yf_ref[:, sl] + y_off + dvec_ref[:, sl] * xs_f
            y = y * _silu(z_ref[:, sl].astype(F32))
            y = y * lax.rsqrt(jnp.mean(y * y, axis=-1, keepdims=True) + RMS_EPS) * ng_ref[:, sl]
            o_ref[:, sl] = y.astype(BF16)


def _expand_mats(off):
    k = (np.arange(2 * LANE) % LANE)[None, :, None]
    col = (np.arange(GROUP_W) // SSD_HEAD_DIM)[None, None, :]
    g = np.arange(N_SSD_GROUPS)[:, None, None]
    return jnp.asarray(k == off + HEADS_PER_GROUP * g + col, dtype=BF16)


def _ssd(xbc3, dt3, bias, alog, h3, dvec, ng):
    bsz, seq, _ = xbc3.shape
    nc = seq // CHUNK
    g_n = N_SSD_GROUPS
    scratch = [pltpu.VMEM((g_n, D_STATE, GROUP_W), F32)]
    sem = ("parallel", "arbitrary")

    def common_specs(cidx):
        return [
            pl.BlockSpec((None, CHUNK, D_INNER), lambda b, j: (b, cidx(j), 0)),
            pl.BlockSpec((None, CHUNK, BC_W), lambda b, j: (b, cidx(j), D_INNER // BC_W)),
            pl.BlockSpec((None, CHUNK, BC_W), lambda b, j: (b, cidx(j), D_INNER // BC_W + 1)),
            pl.BlockSpec((None, CHUNK, LANE), lambda b, j: (b, cidx(j), 0)),
            pl.BlockSpec((1, LANE), lambda b, j: (0, 0)),
            pl.BlockSpec((1, LANE), lambda b, j: (0, 0)),
            pl.BlockSpec((g_n, 2 * LANE, GROUP_W), lambda b, j: (0, 0, 0)),
        ]

    fwd = lambda j: j
    y_f = pl.pallas_call(
        functools.partial(_ssd_kernel, backward=False),
        grid=(bsz, nc),
        in_specs=common_specs(fwd),
        out_specs=pl.BlockSpec((None, CHUNK, D_INNER), lambda b, j: (b, j, 0)),
        out_shape=jax.ShapeDtypeStruct((bsz, seq, D_INNER), F32),
        scratch_shapes=scratch,
        compiler_params=_cparams(sem),
        name="ssd_fwd",
    )(xbc3, xbc3, xbc3, dt3, bias, alog, _expand_mats(0))

    bwd = lambda j: nc - 1 - j
    return pl.pallas_call(
        functools.partial(_ssd_kernel, backward=True),
        grid=(bsz, nc),
        in_specs=common_specs(bwd) + [
            pl.BlockSpec((None, CHUNK, D_INNER), lambda b, j: (b, bwd(j), 0)),
            pl.BlockSpec((None, CHUNK, D_INNER), lambda b, j: (b, bwd(j), Z_BLK0 * COL_BLK // D_INNER)),
            pl.BlockSpec((1, D_INNER), lambda b, j: (0, 0)),
            pl.BlockSpec((1, D_INNER), lambda b, j: (0, 0)),
        ],
        out_specs=pl.BlockSpec((None, CHUNK, D_INNER), lambda b, j: (b, bwd(j), 0)),
        out_shape=jax.ShapeDtypeStruct((bsz, seq, D_INNER), BF16),
        scratch_shapes=scratch,
        compiler_params=_cparams(sem),
        name="ssd_bwd",
    )(xbc3, xbc3, xbc3, dt3, bias, alog, _expand_mats(N_SSD_HEADS), y_f, h3, dvec, ng)


def _branch_mix_kernel(ya_ref, ys_ref, ga_ref, gs_ref, wa_ref, ws_ref, o_ref):
    pa = jax.nn.sigmoid(ga_ref[...].astype(F32)) * _dot(ya_ref[...], wa_ref[...])
    ps = jax.nn.sigmoid(gs_ref[...].astype(F32)) * _dot(ys_ref[...], ws_ref[...])
    o_ref[...] = (pa + ps).astype(BF16)


def _branch_mix(y_attn, y_ssd, gate, wa, ws, *, tm=1024, tn=512):
    m = y_attn.shape[0]
    d = wa.shape[1]
    tm = min(tm, m)
    n_blk = d // tn
    return pl.pallas_call(
        _branch_mix_kernel,
        grid=(m // tm, n_blk),
        in_specs=[
            pl.BlockSpec((tm, Q_W), lambda i, n: (i, 0)),
            pl.BlockSpec((tm, D_INNER), lambda i, n: (i, 0)),
            pl.BlockSpec((tm, tn), lambda i, n: (i, n)),
            pl.BlockSpec((tm, tn), lambda i, n: (i, n_blk + n)),
            pl.BlockSpec((Q_W, tn), lambda i, n: (0, n)),
            pl.BlockSpec((D_INNER, tn), lambda i, n: (0, n)),
        ],
        out_specs=pl.BlockSpec((tm, tn), lambda i, n: (i, n)),
        out_shape=jax.ShapeDtypeStruct((m, d), BF16),
        compiler_params=_cparams(("parallel", "arbitrary")),
        name="branch_mix",
    )(y_attn, y_ssd, gate, gate, wa, ws)


def _proj_ln_kernel(a_ref, x_ref, w_ref, g_ref, b_ref, o_ref, *maybe_ob_ref, scale):
    y = _dot(a_ref[...], w_ref[...])
    if scale != 1.0:
        y = scale * y
    o = _layer_norm(ALPHA * x_ref[...] + y, g_ref[...], b_ref[...])
    o_ref[...] = o
    for ob_ref in maybe_ob_ref:
        ob_ref[...] = o.astype(BF16)


def _proj_ln(a, x, w, g, b, *, scale=1.0, tm=512, emit_bf16=True):
    m, d = x.shape
    k = a.shape[1]
    tm = min(tm, m)
    n_out = 2 if emit_bf16 else 1
    return pl.pallas_call(
        functools.partial(_proj_ln_kernel, scale=scale),
        grid=(m // tm,),
        in_specs=[
            pl.BlockSpec((tm, k), lambda i: (i, 0)),
            pl.BlockSpec((tm, d), lambda i: (i, 0)),
            pl.BlockSpec((k, d), lambda i: (0, 0), pipeline_mode=pl.Buffered(1)),
            pl.BlockSpec((1, d), lambda i: (0, 0)),
            pl.BlockSpec((1, d), lambda i: (0, 0)),
        ],
        out_specs=[pl.BlockSpec((tm, d), lambda i: (i, 0))] * n_out,
        out_shape=[jax.ShapeDtypeStruct((m, d), F32), jax.ShapeDtypeStruct((m, d), BF16)][:n_out],
        compiler_params=_cparams(("parallel",)),
        name="proj_ln",
    )(a, x, w, g, b)


def _cross_kernel(x_ref, xb_ref, wq_ref, kv_ref, wo_ref, g_ref, b_ref, o_ref):
    scale = MEM_HEAD_DIM ** -0.5
    q = (_dot(xb_ref[...], wq_ref[...]) * scale).astype(BF16)
    scores = []
    for hh in range(N_MEM_HEADS):
        sl = slice(hh * MEM_HEAD_DIM, (hh + 1) * MEM_HEAD_DIM)
        scores.append(_dot_nt(q[:, sl], kv_ref[:, sl]))
    outs = []
    for hh in range(N_MEM_HEADS):
        v = kv_ref[:, MEM_W + hh * MEM_HEAD_DIM:MEM_W + (hh + 1) * MEM_HEAD_DIM]
        s = scores[hh]
        p = jnp.exp(s - jnp.max(s, axis=-1, keepdims=True))
        p = p / jnp.sum(p, axis=-1, keepdims=True)
        outs.append(_dot(p.astype(BF16), v))
    o_att = jnp.concatenate(outs, axis=1).astype(BF16)
    y = ALPHA * x_ref[...] + _dot(o_att, wo_ref[...])
    o_ref[...] = _layer_norm(y, g_ref[...], b_ref[...])


def _cross(x2, x2b, wq, kv3, wo, g, b, *, seq, tm=512):
    m, d = x2.shape
    tm = min(tm, seq)
    t_per_seq = seq // tm
    mem_len = kv3.shape[1]
    return pl.pallas_call(
        _cross_kernel,
        grid=(m // tm,),
        in_specs=[
            pl.BlockSpec((tm, d), lambda i: (i, 0)),
            pl.BlockSpec((tm, d), lambda i: (i, 0)),
            pl.BlockSpec((d, MEM_W), lambda i: (0, 0)),
            pl.BlockSpec((None, mem_len, 2 * MEM_W), lambda i: (i // t_per_seq, 0, 0)),
            pl.BlockSpec((MEM_W, d), lambda i: (0, 0)),
            pl.BlockSpec((1, d), lambda i: (0, 0)),
            pl.BlockSpec((1, d), lambda i: (0, 0)),
        ],
        out_specs=pl.BlockSpec((tm, d), lambda i: (i, 0)),
        out_shape=jax.ShapeDtypeStruct((m, d), F32),
        compiler_params=_cparams(("parallel",)),
        name="cross_attn",
    )(x2, x2b, wq, kv3, wo, g, b)


def _rope_tables(seq):
    t = np.arange(seq)
    inv_freq = ROPE_THETA ** (-np.arange(ROPE_AXIS_DIM // 2, dtype=np.float64) * 2.0 / ROPE_AXIS_DIM)
    ang_r = (t // GRID_W)[:, None] * inv_freq[None, :]
    ang_c = (t % GRID_W)[:, None] * inv_freq[None, :]
    cr, sr, cc, sc = np.cos(ang_r), np.sin(ang_r), np.cos(ang_c), np.sin(ang_c)
    rc = np.concatenate([cr, cr, cc, cc], axis=1)
    rs = np.concatenate([-sr, sr, -sc, sc], axis=1)
    lane = np.arange(HEAD_DIM)
    swap = (lane[:, None] == (lane[None, :] ^ (ROPE_AXIS_DIM // 2)))
    swap2 = np.concatenate([swap, swap], axis=0)
    return (jnp.asarray(rc.astype(np.float32)), jnp.asarray(rs.astype(np.float32)),
            jnp.asarray(swap2, dtype=BF16))


def kernel(x, mem, ffn1_w1, ffn1_w3, ffn1_w2, ln1_g, ln1_b, w_in, q_norm_g, k_norm_g, conv_w, conv_b,
           dt_bias, a_log, ssd_d, ssd_norm_g, w_branch_attn, w_branch_ssd, w_mix_out, ln2_g, ln2_b,
           mem_wq, mem_wk, mem_wv, mem_wo, ln3_g, ln3_b, ffn2_w1, ffn2_w3, ffn2_w2, ln4_g, ln4_b):
    bsz, seq, d = x.shape
    m = bsz * seq
    mem_len = mem.shape[1]
    bf = lambda t: t.astype(BF16)
    rc, rs, swap = _rope_tables(seq)
    xc = x.reshape(m, d)
    for l in range(DEPTH):
        row = lambda v: v[l].reshape(1, -1)
        later = [ffn1_w2[l], w_in[l], w_branch_attn[l], w_branch_ssd[l], w_mix_out[l],
                 ffn2_w1[l], ffn2_w3[l], ffn2_w2[l]]
        act, f1w2, w_all, w_ba, w_bs, w_mo, f2w1, f2w3, f2w2 = _glu(
            xc, bf(ffn1_w1[l]), bf(ffn1_w3[l]), cast=later)
        x1, x1b = _proj_ln(act, xc, f1w2, row(ln1_g), row(ln1_b), scale=0.5, tm=256)
        dt0 = Q_W + 2 * KV_W + D_INNER + XBC_W
        gains = jnp.concatenate([jnp.tile(q_norm_g[l] * (HEAD_DIM ** -0.5 * LOG2E), N_Q_HEADS),
                                 jnp.tile(k_norm_g[l], N_KV_HEADS)]).reshape(1, -1)
        qk = _qk_proj(x1b, w_all, gains, rc, rs, swap, seq=seq)
        h = _matmul(x1b, w_all, BF16, tm=2048, tn=COL_BLK, n_out=H_W, col0=SRC_ROPE_BLKS * COL_BLK,
                    out_rot=1)
        dt_raw = _matmul(x1b, w_all, F32, tm=1024, tn=DT_W, n_out=DT_W, col0=dt0)
        gate = _matmul(x1b, w_all, BF16, tm=2048, tn=COL_BLK, n_out=2 * D_MODEL, col0=dt0 + DT_W,
                       strips=COL_BLK // LANE)
        h3 = h.reshape(bsz, seq, H_W)
        score_bound = HEAD_DIM * jnp.max(jnp.abs(gains[0, :Q_W])) * jnp.max(jnp.abs(gains[0, Q_W:]))
        y_attn = _attention(qk.reshape(bsz, seq, SRC_ROPE_BLKS * COL_BLK), h3,
                            (score_bound < SCORE_SAFE_LOG2).astype(jnp.int32).reshape(1))
        xbc = _conv_silu(h3, conv_w[l], conv_b[l].reshape(1, -1))
        y_ssd = _ssd(xbc, dt_raw.reshape(bsz, seq, DT_W), dt_bias[l].reshape(1, DT_W),
                     a_log[l].reshape(1, DT_W), h3,
                     jnp.repeat(ssd_d[l], SSD_HEAD_DIM).reshape(1, D_INNER),
                     ssd_norm_g[l].reshape(1, D_INNER))
        merged = _branch_mix(y_attn.reshape(m, Q_W), y_ssd.reshape(m, D_INNER), gate,
                             w_ba, w_bs)
        x2, x2b = _proj_ln(merged, x1, w_mo, row(ln2_g), row(ln2_b))
        w_kv = bf(jnp.concatenate([mem_wk[l], mem_wv[l]], axis=1))
        kv = _matmul(bf(mem.reshape(bsz * mem_len, d)), w_kv, BF16, tm=bsz * mem_len, tn=COL_BLK)
        x3 = _cross(x2, x2b, bf(mem_wq[l]), kv.reshape(bsz, mem_len, 2 * MEM_W), bf(mem_wo[l]),
                       row(ln3_g), row(ln3_b), seq=seq)
        (act,) = _glu(x3, f2w1, f2w3)
        (xc,) = _proj_ln(act, x3, f2w2, row(ln4_g), row(ln4_b), scale=0.5, tm=256,
                         emit_bf16=False)
    return xc.reshape(bsz, seq, d)
```
